```python
import jax, jax.numpy as jnp
from jax import lax
import numpy as np

D_MODEL = 1024
BATCH = 4
SEQ = 4096
DEPTH = 2

N_HEADS = 8
HEAD_DIM = 128
ATTN_WIDTH = N_HEADS * HEAD_DIM
IDX_HEADS = 16
IDX_DIM = 64
TOPK_MAX = 256
Q_BLOCK = 128
D_RNN = 1408
RNN_BLOCKS = 16
RNN_BLOCK_DIM = D_RNN // RNN_BLOCKS
CONV_WIDTH = 4
LRU_C = 8.0
ROPE_THETA = 10000.0
NORM_EPS = 1e-6

SPLIT_SIZES = (ATTN_WIDTH, ATTN_WIDTH, ATTN_WIDTH, ATTN_WIDTH,
               IDX_HEADS * IDX_DIM, IDX_DIM, IDX_HEADS,
               D_RNN, D_RNN, D_MODEL, D_MODEL)
N_IN = sum(SPLIT_SIZES)

kernel_name = "hybrid_dsa_rglru_gated_parallel"


def rmsnorm(x, g):
    xf = x.astype(jnp.float32)
    y = xf * lax.rsqrt(jnp.mean(xf * xf, axis=-1, keepdims=True) + NORM_EPS)
    return (y * g.astype(jnp.float32)).astype(x.dtype)


def rope_tables(positions, dim):
    inv = ROPE_THETA ** (-jnp.arange(0, dim, 2, dtype=jnp.float32) / dim)
    ang = positions.astype(jnp.float32)[..., None] * inv
    return jnp.cos(ang), jnp.sin(ang)


def apply_rope(x, cos, sin):
    extra = x.ndim - cos.ndim
    shape = cos.shape[:2] + (1,) * extra + cos.shape[2:]
    c, s = cos.reshape(shape), sin.reshape(shape)
    x1, x2 = jnp.split(x.astype(jnp.float32), 2, axis=-1)
    return jnp.concatenate([x1 * c - x2 * s, x2 * c + x1 * s], axis=-1).astype(x.dtype)


def dsa_attention(q, k, v, iq, ik, iw):
    B, S, H, Dh = q.shape
    topk = min(TOPK_MAX, S // 4)
    n_blk = S // Q_BLOCK
    key_pos = jnp.arange(S)
    scale = HEAD_DIM ** -0.5
    iw = iw.astype(jnp.float32) * (IDX_HEADS ** -0.5) * (IDX_DIM ** -0.5)

    def block(i):
        q0 = i * Q_BLOCK
        qb = lax.dynamic_slice_in_dim(q, q0, Q_BLOCK, axis=1)
        iqb = lax.dynamic_slice_in_dim(iq, q0, Q_BLOCK, axis=1)
        iwb = lax.dynamic_slice_in_dim(iw, q0, Q_BLOCK, axis=1)
        qpos = q0 + jnp.arange(Q_BLOCK)
        causal = key_pos[None, :] <= qpos[:, None]
        dots = jnp.einsum('bqhd,bsd->bqhs', iqb, ik, preferred_element_type=jnp.float32)
        idx_score = jnp.einsum('bqh,bqhs->bqs', iwb, jax.nn.relu(dots))
        idx_score = jnp.where(causal[None], idx_score, -jnp.inf)
        _, sel = lax.top_k(idx_score, topk)
        valid = sel <= qpos[None, :, None]
        k_sel = jax.vmap(lambda kb, ib: kb[ib])(k, sel)
        v_sel = jax.vmap(lambda vb, ib: vb[ib])(v, sel)
        logits = jnp.einsum('bqhd,bqkhd->bhqk', qb, k_sel, preferred_element_type=jnp.float32) * scale
        logits = jnp.where(valid[:, None], logits, -jnp.inf)
        p = jax.nn.softmax(logits, axis=-1)
        return jnp.einsum('bhqk,bqkhd->bqhd', p.astype(v.dtype), v_sel)

    out = lax.map(block, jnp.arange(n_blk))
    return out.transpose(1, 0, 2, 3, 4).reshape(B, S, H, Dh)


def causal_depthwise_conv(x, w, b):
    y = lax.conv_general_dilated(
        x, w[:, None, :].astype(x.dtype), window_strides=(1,),
        padding=[(CONV_WIDTH - 1, 0)], dimension_numbers=('NWC', 'WIO', 'NWC'),
        feature_group_count=x.shape[-1])
    return y + b.astype(x.dtype)


def rg_lru(x, w_r, b_r, w_i, b_i, lam):
    B, S, _ = x.shape
    xb = x.reshape(B, S, RNN_BLOCKS, RNN_BLOCK_DIM)
    r = jax.nn.sigmoid((jnp.einsum('bsnc,ncd->bsnd', xb, w_r).reshape(B, S, D_RNN) + b_r).astype(jnp.float32))
    i = jax.nn.sigmoid((jnp.einsum('bsnc,ncd->bsnd', xb, w_i).reshape(B, S, D_RNN) + b_i).astype(jnp.float32))
    log_a = -LRU_C * r * jax.nn.softplus(-lam.astype(jnp.float32))
    a = jnp.exp(log_a)
    u = jnp.sqrt(-jnp.expm1(2.0 * log_a)) * (i * x.astype(jnp.float32))

    def combine(left, right):
        a_l, b_l = left
        a_r, b_r = right
        return a_l * a_r, a_r * b_l + b_r

    _, h = lax.associative_scan(combine, (a, u), axis=1)
    return h.astype(x.dtype)


def setup_inputs(seed: int = 0) -> dict:
    key = jax.random.key(seed)
    ks = jax.random.split(key, 16)
    f32 = jnp.float32
    x = jax.random.normal(ks[0], (BATCH, SEQ, D_MODEL), f32)
    positions = jnp.broadcast_to(jnp.arange(SEQ, dtype=jnp.int32), (BATCH, SEQ))
    norm_g = 1.0 + 0.02 * jax.random.normal(ks[1], (DEPTH, D_MODEL), f32)
    w_in = jax.random.normal(ks[2], (DEPTH, D_MODEL, N_IN), f32) * D_MODEL ** -0.5
    conv_w = jax.random.normal(ks[3], (DEPTH, CONV_WIDTH, D_RNN), f32) * CONV_WIDTH ** -0.5
    conv_b = 0.01 * jax.random.normal(ks[4], (DEPTH, D_RNN), f32)
    w_rg = jax.random.normal(ks[5], (DEPTH, RNN_BLOCKS, RNN_BLOCK_DIM, RNN_BLOCK_DIM), f32) * RNN_BLOCK_DIM ** -0.5
    b_rg = 0.01 * jax.random.normal(ks[6], (DEPTH, D_RNN), f32)
    w_ig = jax.random.normal(ks[7], (DEPTH, RNN_BLOCKS, RNN_BLOCK_DIM, RNN_BLOCK_DIM), f32) * RNN_BLOCK_DIM ** -0.5
    b_ig = 0.01 * jax.random.normal(ks[8], (DEPTH, D_RNN), f32)
    a0 = jax.random.uniform(ks[9], (DEPTH, D_RNN), f32, 0.9, 0.999)
    p = a0 ** (1.0 / LRU_C)
    lru_lambda = jnp.log(p) - jnp.log1p(-p)
    w_out_attn = jax.random.normal(ks[10], (DEPTH, ATTN_WIDTH, D_MODEL), f32) * ATTN_WIDTH ** -0.5
    w_out_rnn = jax.random.normal(ks[11], (DEPTH, D_RNN, D_MODEL), f32) * D_RNN ** -0.5
    w_o = jax.random.normal(ks[12], (DEPTH, D_MODEL, D_MODEL), f32) * D_MODEL ** -0.5
    final_g = 1.0 + 0.02 * jax.random.normal(ks[13], (D_MODEL,), f32)
    return {"x": x, "positions": positions, "norm_g": norm_g, "w_in": w_in,
            "conv_w": conv_w, "conv_b": conv_b, "w_rg": w_rg, "b_rg": b_rg,
            "w_ig": w_ig, "b_ig": b_ig, "lru_lambda": lru_lambda,
            "w_out_attn": w_out_attn, "w_out_rnn": w_out_rnn, "w_o": w_o,
            "final_g": final_g}


def reference(x, positions, norm_g, w_in, conv_w, conv_b, w_rg, b_rg, w_ig, b_ig,
              lru_lambda, w_out_attn, w_out_rnn, w_o, final_g):
    B, S, _ = x.shape
    offsets = np.cumsum(SPLIT_SIZES)[:-1].tolist()
    cos_a, sin_a = rope_tables(positions, HEAD_DIM)
    cos_i, sin_i = rope_tables(positions, IDX_DIM)
    for l in range(DEPTH):
        h = rmsnorm(x, norm_g[l])
        proj = h @ w_in[l]
        q, k, v, ga, iq, ik, iw, xr, gr, ma, mb = jnp.split(proj, offsets, axis=-1)
        q = apply_rope(q.reshape(B, S, N_HEADS, HEAD_DIM), cos_a, sin_a)
        k = apply_rope(k.reshape(B, S, N_HEADS, HEAD_DIM), cos_a, sin_a)
        v = v.reshape(B, S, N_HEADS, HEAD_DIM)
        iq = apply_rope(iq.reshape(B, S, IDX_HEADS, IDX_DIM), cos_i, sin_i)
        ik = apply_rope(ik, cos_i, sin_i)
        attn = dsa_attention(q, k, v, iq, ik, iw).reshape(B, S, ATTN_WIDTH)
        y_a = (attn * jax.nn.silu(ga)) @ w_out_attn[l]
        xr = causal_depthwise_conv(xr, conv_w[l], conv_b[l])
        hr = rg_lru(xr, w_rg[l], b_rg[l], w_ig[l], b_ig[l], lru_lambda[l])
        y_b = (hr * jax.nn.silu(gr)) @ w_out_rnn[l]
        merged = jax.nn.sigmoid(ma) * y_a + jax.nn.sigmoid(mb) * y_b
        x = x + merged @ w_o[l]
    return rmsnorm(x, final_g)
```

```python
import functools

import jax
import jax.numpy as jnp
import numpy as np
from jax import lax
from jax.experimental import pallas as pl
from jax.experimental.pallas import tpu as pltpu

D_MODEL = 1024
N_HEADS = 8
HEAD_DIM = 128
ATTN_WIDTH = N_HEADS * HEAD_DIM
IDX_HEADS = 16
IDX_DIM = 64
TOPK_MAX = 256
Q_BLOCK = 128
D_RNN = 1408
RNN_BLOCKS = 16
RNN_BLOCK_DIM = D_RNN // RNN_BLOCKS
CONV_WIDTH = 4
LRU_C = 8.0
ROPE_THETA = 10000.0
NORM_EPS = 1e-6

LANES = 128
SUBLANES = 8
VMEM_LIMIT = 56 * 1024 * 1024

F32 = jnp.float32
BF16 = jnp.bfloat16
NEG_INF = float("-inf")


def _cparams(n_grid):
    return pltpu.CompilerParams(
        dimension_semantics=("arbitrary",) * n_grid,
        vmem_limit_bytes=VMEM_LIMIT)


def _rope_table_kernel(pos_ref, inv_ref, sgn_ref, cos_ref, sin_ref):
    ang = pos_ref[...].astype(F32) * inv_ref[...]
    cos_ref[...] = jnp.cos(ang)
    sin_ref[...] = jnp.sin(ang) * sgn_ref[...]


def _rope_tables(positions):
    m = positions.size
    inv_a = ROPE_THETA ** (-jnp.arange(0, HEAD_DIM, 2, dtype=F32) / HEAD_DIM)
    inv_i = ROPE_THETA ** (-jnp.arange(0, IDX_DIM, 2, dtype=F32) / IDX_DIM)
    inv = jnp.concatenate([inv_a, inv_a, inv_i, inv_i, inv_i, inv_i])[None, :]
    ha, hi = HEAD_DIM // 2, IDX_DIM // 2
    sgn = np.concatenate([-np.ones(ha), np.ones(ha),
                          -np.ones(hi), np.ones(hi), -np.ones(hi), np.ones(hi)])
    sgn = jnp.asarray(sgn, F32)[None, :]
    tm = 2048
    return pl.pallas_call(
        _rope_table_kernel,
        grid=(m // tm,),
        in_specs=[pl.BlockSpec((tm, 1), lambda i: (i, 0)),
                  pl.BlockSpec((1, 2 * LANES), lambda i: (0, 0)),
                  pl.BlockSpec((1, 2 * LANES), lambda i: (0, 0))],
        out_specs=[pl.BlockSpec((tm, 2 * LANES), lambda i: (i, 0))] * 2,
        out_shape=[jax.ShapeDtypeStruct((m, 2 * LANES), F32)] * 2,
        compiler_params=_cparams(1),
        name="rope_tables",
    )(positions.reshape(m, 1), inv, sgn)


def _rmsnorm_kernel(x_ref, g_ref, o_ref):
    x = x_ref[...]
    y = x * lax.rsqrt(jnp.mean(x * x, axis=-1, keepdims=True) + NORM_EPS)
    o_ref[...] = (y * g_ref[...]).astype(o_ref.dtype)


def _rmsnorm(x2d, g, out_dtype):
    m, d = x2d.shape
    tm = 1024
    return pl.pallas_call(
        _rmsnorm_kernel,
        grid=(m // tm,),
        in_specs=[pl.BlockSpec((tm, d), lambda i: (i, 0)),
                  pl.BlockSpec((1, d), lambda i: (0, 0))],
        out_specs=pl.BlockSpec((tm, d), lambda i: (i, 0)),
        out_shape=jax.ShapeDtypeStruct((m, d), out_dtype),
        compiler_params=_cparams(1),
        name="rmsnorm",
    )(x2d, g.reshape(1, d))


ROPE_TN = 512


def _rope128(x, cos, sin):
    return x * cos + pltpu.roll(x, HEAD_DIM // 2, 1) * sin


def _rope64(x, cos, sin, first_half):
    half = IDX_DIM // 2
    partner = jnp.where(first_half, pltpu.roll(x, LANES - half, 1),
                        pltpu.roll(x, half, 1))
    return x * cos + partner * sin


def _proj_rope_kernel(h_ref, w_ref, cos_ref, sin_ref, o_ref):
    j = pl.program_id(1)
    acc = jnp.dot(h_ref[...], w_ref[...], preferred_element_type=F32)
    n_sub = ROPE_TN // LANES
    cos_a, sin_a = cos_ref[:, :LANES], sin_ref[:, :LANES]
    cos_i, sin_i = cos_ref[:, LANES:], sin_ref[:, LANES:]
    lane = lax.broadcasted_iota(jnp.int32, (1, LANES), 1)
    first_half = (lane % IDX_DIM) < (IDX_DIM // 2)

    @pl.when(j < 4)
    def _():
        scale = jnp.where(j < 2, HEAD_DIM ** -0.5, 1.0).astype(F32)
        for c in range(n_sub):
            x = acc[:, c * LANES:(c + 1) * LANES]
            o_ref[:, c * LANES:(c + 1) * LANES] = (
                _rope128(x, cos_a, sin_a) * scale).astype(o_ref.dtype)

    @pl.when(jnp.logical_and(j >= 4, j < 6))
    def _():
        for c in range(n_sub):
            x = acc[:, c * LANES:(c + 1) * LANES]
            o_ref[:, c * LANES:(c + 1) * LANES] = _rope64(
                x, cos_i, sin_i, first_half).astype(o_ref.dtype)

    @pl.when(j == 6)
    def _():
        for c in range(n_sub):
            x = acc[:, c * LANES:(c + 1) * LANES]
            if c < 2:
                x = _rope64(x, cos_i, sin_i, first_half)
            o_ref[:, c * LANES:(c + 1) * LANES] = x.astype(o_ref.dtype)


def _proj_rope(h, w, cos, sin):
    m, d = h.shape
    n = w.shape[1]
    tm = 1024
    return pl.pallas_call(
        _proj_rope_kernel,
        grid=(m // tm, n // ROPE_TN),
        in_specs=[pl.BlockSpec((tm, d), lambda i, j: (i, 0)),
                  pl.BlockSpec((d, ROPE_TN), lambda i, j: (0, j)),
                  pl.BlockSpec((tm, 2 * LANES), lambda i, j: (i, 0)),
                  pl.BlockSpec((tm, 2 * LANES), lambda i, j: (i, 0))],
        out_specs=pl.BlockSpec((tm, ROPE_TN), lambda i, j: (i, j)),
        out_shape=jax.ShapeDtypeStruct((m, n), BF16),
        compiler_params=_cparams(2),
        name="proj_rope",
    )(h, w, cos, sin)


def _proj_plain_kernel(h_ref, w_ref, o_ref):
    o_ref[...] = jnp.dot(h_ref[...], w_ref[...],
                         preferred_element_type=F32).astype(o_ref.dtype)


def _proj_plain(h, w, tn, name):
    m, d = h.shape
    n = w.shape[1]
    tm = 1024
    return pl.pallas_call(
        _proj_plain_kernel,
        grid=(m // tm, n // tn),
        in_specs=[pl.BlockSpec((tm, d), lambda i, j: (i, 0)),
                  pl.BlockSpec((d, tn), lambda i, j: (0, j))],
        out_specs=pl.BlockSpec((tm, tn), lambda i, j: (i, j)),
        out_shape=jax.ShapeDtypeStruct((m, n), BF16),
        compiler_params=_cparams(2),
        name=name,
    )(h, w)


ATT_TK = 512


def _dsa_kernel(q_ref, iq_ref, k_ref, ikw_ref, v_ref, ga_ref, o_ref,
                sc_ref, bias_ref, lg_ref, *, topk):
    i = pl.program_id(1)
    tk = ATT_TK
    nk = i // (tk // Q_BLOCK) + 1
    q0 = pl.multiple_of(i * Q_BLOCK, Q_BLOCK)
    rows = q0 + lax.broadcasted_iota(jnp.int32, (Q_BLOCK, 1), 0)

    iq = iq_ref[...]
    n_pair = IDX_HEADS // 2
    lhs = jnp.concatenate(
        [iq[:, p * LANES:(p + 1) * LANES] for p in range(n_pair)], axis=0)
    w = ikw_ref[pl.ds(q0, Q_BLOCK), 2 * LANES:3 * LANES].astype(F32)
    w = w * (IDX_HEADS ** -0.5) * (IDX_DIM ** -0.5)
    wcol = [w[:, h:h + 1] for h in range(IDX_HEADS)]

    def score_tile(t, carry):
        k0 = pl.multiple_of(t * tk, tk)
        acc = jnp.zeros((Q_BLOCK, tk), F32)
        for e in range(2):
            ik_e = ikw_ref[pl.ds(k0, tk), e * LANES:(e + 1) * LANES]
            d = lax.dot_general(lhs, ik_e, (((1,), (1,)), ((), ())),
                                preferred_element_type=F32)
            for p in range(n_pair):
                dp = d[p * Q_BLOCK:(p + 1) * Q_BLOCK, :]
                acc = acc + wcol[2 * p + e] * jnp.maximum(dp, 0.0)
        cols = k0 + lax.broadcasted_iota(jnp.int32, (1, tk), 1)
        sc_ref[:, pl.ds(k0, tk)] = jnp.where(cols <= rows, acc, NEG_INF)
        return carry

    lax.fori_loop(0, nk, score_tile, 0)

    int_min = jnp.int32(-2 ** 31)

    def key_to_float(t_u):
        key = t_u ^ int_min
        bits = jnp.where(key >= 0, key, key ^ jnp.int32(0x7FFFFFFF))
        return pltpu.bitcast(bits, F32)

    def bisect(it, t_u):
        cand_u = t_u | jnp.left_shift(jnp.int32(1), 31 - it)
        cand_f = key_to_float(cand_u)

        def count_tile(t, cnt):
            k0 = pl.multiple_of(t * tk, tk)
            s = sc_ref[:, pl.ds(k0, tk)]
            for c in range(tk // LANES):
                cnt = cnt + jnp.where(
                    s[:, c * LANES:(c + 1) * LANES] >= cand_f, 1.0, 0.0)
            return cnt

        cnt = lax.fori_loop(0, nk, count_tile,
                            jnp.zeros((Q_BLOCK, LANES), F32))
        total = jnp.sum(cnt, axis=1, keepdims=True)
        return jnp.where(total >= topk, cand_u, t_u)

    t_u = lax.fori_loop(0, 32, bisect, jnp.zeros((Q_BLOCK, 1), jnp.int32))
    thr = jnp.where(rows < topk, NEG_INF, key_to_float(t_u))

    def bias_tile(t, carry):
        k0 = pl.multiple_of(t * tk, tk)
        s = sc_ref[:, pl.ds(k0, tk)]
        cols = k0 + lax.broadcasted_iota(jnp.int32, (1, tk), 1)
        causal_bias = jnp.where(cols <= rows, 0.0, NEG_INF)
        bias_ref[:, pl.ds(k0, tk)] = jnp.where(s >= thr, causal_bias, NEG_INF)
        return carry

    lax.fori_loop(0, nk, bias_tile, 0)

    for h in range(N_HEADS):
        hs = slice(h * HEAD_DIM, (h + 1) * HEAD_DIM)
        qh = q_ref[:, hs]

        def logits_tile(t, mx):
            k0 = pl.multiple_of(t * tk, tk)
            s = lax.dot_general(qh, k_ref[pl.ds(k0, tk), hs],
                                (((1,), (1,)), ((), ())),
                                preferred_element_type=F32)
            s = s + bias_ref[:, pl.ds(k0, tk)]
            lg_ref[:, pl.ds(k0, tk)] = s
            for c in range(tk // LANES):
                mx = jnp.maximum(mx, s[:, c * LANES:(c + 1) * LANES])
            return mx

        mx = lax.fori_loop(0, nk, logits_tile,
                           jnp.full((Q_BLOCK, LANES), NEG_INF, F32))
        m = jnp.max(mx, axis=1, keepdims=True)

        def pv_tile(t, carry):
            l, acc = carry
            k0 = pl.multiple_of(t * tk, tk)
            p = jnp.exp(lg_ref[:, pl.ds(k0, tk)] - m)
            for c in range(tk // LANES):
                l = l + p[:, c * LANES:(c + 1) * LANES]
            acc = acc + jnp.dot(p.astype(BF16), v_ref[pl.ds(k0, tk), hs],
                                preferred_element_type=F32)
            return l, acc

        l, acc = lax.fori_loop(
            0, nk, pv_tile,
            (jnp.zeros((Q_BLOCK, LANES), F32),
             jnp.zeros((Q_BLOCK, HEAD_DIM), F32)))
        out = acc / jnp.sum(l, axis=1, keepdims=True)
        gate = ga_ref[:, hs].astype(F32)
        o_ref[:, hs] = (out * (gate * jax.nn.sigmoid(gate))).astype(o_ref.dtype)


def _dsa_attention(pr, pv, b, s):
    pr3 = pr.reshape(b, s, pr.shape[1])
    pv3 = pv.reshape(b, s, pv.shape[1])
    topk = min(TOPK_MAX, s // 4)
    w_att = ATTN_WIDTH
    single = pl.Buffered(1)
    return pl.pallas_call(
        functools.partial(_dsa_kernel, topk=topk),
        grid=(b, s // Q_BLOCK),
        in_specs=[
            pl.BlockSpec((None, Q_BLOCK, w_att), lambda bi, i: (bi, i, 0)),
            pl.BlockSpec((None, Q_BLOCK, w_att), lambda bi, i: (bi, i, 2)),
            pl.BlockSpec((None, s, w_att), lambda bi, i: (bi, 0, 1),
                         pipeline_mode=single),
            pl.BlockSpec((None, s, 3 * LANES), lambda bi, i: (bi, 0, 8),
                         pipeline_mode=single),
            pl.BlockSpec((None, s, w_att), lambda bi, i: (bi, 0, 0),
                         pipeline_mode=single),
            pl.BlockSpec((None, Q_BLOCK, w_att), lambda bi, i: (bi, i, 1)),
        ],
        out_specs=pl.BlockSpec((None, Q_BLOCK, w_att), lambda bi, i: (bi, i, 0)),
        out_shape=jax.ShapeDtypeStruct((b, s, w_att), BF16),
        scratch_shapes=[pltpu.VMEM((Q_BLOCK, s), F32)] * 3,
        compiler_params=_cparams(2),
        name="dsa_attention",
    )(pr3, pr3, pr3, pr3, pv3, pv3)


RNN_T = 512


def _rglru_kernel(xr_ref, gr_ref, cw_ref, cb_ref, wg_ref, bg_ref, lam_ref,
                  o_ref, xp_ref, a_ref, u_ref, h_ref, hc_ref):
    c = pl.program_id(1)
    t_len = RNN_T
    pad = SUBLANES

    @pl.when(c == 0)
    def _():
        xp_ref[0:pad, :] = jnp.zeros((pad, D_RNN), F32)
        hc_ref[...] = jnp.zeros((1, D_RNN), F32)

    x = xr_ref[...].astype(F32)
    xp_ref[pad:pad + t_len, :] = x
    y = cb_ref[...] + cw_ref[CONV_WIDTH - 1:CONV_WIDTH, :] * x
    for j in range(CONV_WIDTH - 1):
        back = CONV_WIDTH - 1 - j
        y = y + cw_ref[j:j + 1, :] * xp_ref[pad - back:pad - back + t_len, :]
    xp_ref[0:pad, :] = x[t_len - pad:, :]

    g = jnp.dot(y.astype(BF16), wg_ref[...],
                preferred_element_type=F32) + bg_ref[...]
    r = jax.nn.sigmoid(g[:, :D_RNN])
    gi = jax.nn.sigmoid(g[:, D_RNN:])
    z = -lam_ref[...]
    softplus = jnp.maximum(z, 0.0) + jnp.log1p(jnp.exp(-jnp.abs(z)))
    log_a = -LRU_C * r * softplus
    a = jnp.exp(log_a)
    a_ref[...] = a
    u_ref[...] = jnp.sqrt(1.0 - a * a) * (gi * y)

    def step(t, h):
        h = a_ref[pl.ds(t, 1), :] * h + u_ref[pl.ds(t, 1), :]
        h_ref[pl.ds(t, 1), :] = h
        return h

    hc_ref[...] = lax.fori_loop(0, t_len, step, hc_ref[...], unroll=8)
    gr = gr_ref[...].astype(F32)
    o_ref[...] = (h_ref[...] * (gr * jax.nn.sigmoid(gr))).astype(o_ref.dtype)


def _rglru(pc, b, s, conv_w, conv_b, w_gates, b_gates, lam):
    pc3 = pc.reshape(b, s, pc.shape[1])
    t = RNN_T
    const = lambda bi, c: (0, 0)
    return pl.pallas_call(
        _rglru_kernel,
        grid=(b, s // t),
        in_specs=[
            pl.BlockSpec((None, t, D_RNN), lambda bi, c: (bi, c, 0)),
            pl.BlockSpec((None, t, D_RNN), lambda bi, c: (bi, c, 1)),
            pl.BlockSpec((CONV_WIDTH, D_RNN), const),
            pl.BlockSpec((1, D_RNN), const),
            pl.BlockSpec((D_RNN, 2 * D_RNN), const),
            pl.BlockSpec((1, 2 * D_RNN), const),
            pl.BlockSpec((1, D_RNN), const),
        ],
        out_specs=pl.BlockSpec((None, t, D_RNN), lambda bi, c: (bi, c, 0)),
        out_shape=jax.ShapeDtypeStruct((b, s, D_RNN), BF16),
        scratch_shapes=[pltpu.VMEM((t + SUBLANES, D_RNN), F32),
                        pltpu.VMEM((t, D_RNN), F32),
                        pltpu.VMEM((t, D_RNN), F32),
                        pltpu.VMEM((t, D_RNN), F32),
                        pltpu.VMEM((1, D_RNN), F32)],
        compiler_params=_cparams(2),
        name="rglru",
    )(pc3, pc3, conv_w, conv_b.reshape(1, D_RNN), w_gates,
      b_gates.reshape(1, 2 * D_RNN), lam.reshape(1, D_RNN))


def _merge_kernel(at_ref, hr_ref, ma_ref, mb_ref, x_ref, woa_ref, wor_ref,
                  wo_ref, g_ref, xo_ref, no_ref):
    y_a = jnp.dot(at_ref[...], woa_ref[...], preferred_element_type=F32)
    y_b = jnp.dot(hr_ref[...], wor_ref[...], preferred_element_type=F32)
    merged = (jax.nn.sigmoid(ma_ref[...].astype(F32)) * y_a
              + jax.nn.sigmoid(mb_ref[...].astype(F32)) * y_b)
    x = x_ref[...] + jnp.dot(merged.astype(BF16), wo_ref[...],
                             preferred_element_type=F32)
    xo_ref[...] = x
    y = x * lax.rsqrt(jnp.mean(x * x, axis=-1, keepdims=True) + NORM_EPS)
    no_ref[...] = (y * g_ref[...]).astype(no_ref.dtype)


def _merge(attn_g, hr_g, pv, x2d, w_oa, w_or, w_o, g, norm_dtype):
    m, d = x2d.shape
    tm = 512
    const = lambda i: (0, 0)
    return pl.pallas_call(
        _merge_kernel,
        grid=(m // tm,),
        in_specs=[
            pl.BlockSpec((tm, ATTN_WIDTH), lambda i: (i, 0)),
            pl.BlockSpec((tm, D_RNN), lambda i: (i, 0)),
            pl.BlockSpec((tm, d), lambda i: (i, 2)),
            pl.BlockSpec((tm, d), lambda i: (i, 3)),
            pl.BlockSpec((tm, d), lambda i: (i, 0)),
            pl.BlockSpec((ATTN_WIDTH, d), const),
            pl.BlockSpec((D_RNN, d), const),
            pl.BlockSpec((d, d), const),
            pl.BlockSpec((1, d), const),
        ],
        out_specs=[pl.BlockSpec((tm, d), lambda i: (i, 0))] * 2,
        out_shape=[jax.ShapeDtypeStruct((m, d), F32),
                   jax.ShapeDtypeStruct((m, d), norm_dtype)],
        compiler_params=_cparams(1),
        name="merge",
    )(attn_g, hr_g, pv, pv, x2d, w_oa, w_or, w_o, g.reshape(1, d))


def _pack_weights(w_in_l, w_rg_l, w_ig_l):
    offs = np.cumsum((ATTN_WIDTH,) * 4 + (IDX_HEADS * IDX_DIM, IDX_DIM, IDX_HEADS,
                                          D_RNN, D_RNN, D_MODEL, D_MODEL))[:-1]
    q, k, v, ga, iq, ik, iw, xr, gr, ma, mb = jnp.split(w_in_l, offs.tolist(), axis=1)
    d = w_in_l.shape[0]
    z = lambda n: jnp.zeros((d, n), w_in_l.dtype)
    tail = jnp.concatenate([ik, z(LANES - IDX_DIM), z(LANES - IDX_DIM), ik,
                            iw, z(ROPE_TN - 2 * LANES - IDX_HEADS)], axis=1)
    w_rope = jnp.concatenate([q, k, iq, tail], axis=1).astype(BF16)
    w_val = jnp.concatenate([v, ga, ma, mb], axis=1).astype(BF16)
    w_rnn = jnp.concatenate([xr, gr], axis=1).astype(BF16)
    w_gates = jnp.concatenate([jax.scipy.linalg.block_diag(*w_rg_l),
                               jax.scipy.linalg.block_diag(*w_ig_l)],
                              axis=1).astype(BF16)
    return w_rope, w_val, w_rnn, w_gates


def kernel(x, positions, norm_g, w_in, conv_w, conv_b, w_rg, b_rg, w_ig, b_ig,
           lru_lambda, w_out_attn, w_out_rnn, w_o, final_g):
    b, s, d = x.shape
    depth = w_in.shape[0]
    m = b * s
    cos, sin = _rope_tables(positions)
    x2d = x.reshape(m, d)
    h = _rmsnorm(x2d, norm_g[0], BF16)
    out = None
    for l in range(depth):
        w_rope, w_val, w_rnn, w_gates = _pack_weights(w_in[l], w_rg[l], w_ig[l])
        pr = _proj_rope(h, w_rope, cos, sin)
        pv = _proj_plain(h, w_val, 1024, "proj_val")
        pc = _proj_plain(h, w_rnn, D_RNN, "proj_rnn")
        attn_g = _dsa_attention(pr, pv, b, s).reshape(m, ATTN_WIDTH)
        hr_g = _rglru(pc, b, s, conv_w[l], conv_b[l], w_gates,
                      jnp.concatenate([b_rg[l], b_ig[l]]), lru_lambda[l]
                      ).reshape(m, D_RNN)
        last = l == depth - 1
        g_next = final_g if last else norm_g[l + 1]
        x2d, normed = _merge(attn_g, hr_g, pv, x2d,
                             w_out_attn[l].astype(BF16), w_out_rnn[l].astype(BF16),
                             w_o[l].astype(BF16), g_next, F32 if last else BF16)
        if last:
            out = normed
        else:
            h = normed
    return out.reshape(b, s, d)
```

```python
import functools

import jax
import jax.numpy as jnp
import numpy as np
from jax import lax
from jax.experimental import pallas as pl
from jax.experimental.pallas import tpu as pltpu

D_MODEL = 1024
N_HEADS = 8
HEAD_DIM = 128
ATTN_WIDTH = N_HEADS * HEAD_DIM
IDX_HEADS = 16
IDX_DIM = 64
TOPK_MAX = 256
Q_BLOCK = 128
D_RNN = 1408
RNN_BLOCKS = 16
RNN_BLOCK_DIM = D_RNN // RNN_BLOCKS
CONV_WIDTH = 4
LRU_C = 8.0
ROPE_THETA = 10000.0
NORM_EPS = 1e-6

LANES = 128
SUBLANES = 8
PACKED_SUBLANES = 16
VMEM_LIMIT = 56 * 1024 * 1024

F32 = jnp.float32
BF16 = jnp.bfloat16
NEG_INF = float("-inf")
NT_DIMS = (((1,), (1,)), ((), ()))


def _cparams(n_grid):
    return pltpu.CompilerParams(
        dimension_semantics=("arbitrary",) * n_grid,
        vmem_limit_bytes=VMEM_LIMIT)


def _rope_table_kernel(pos_ref, inv_ref, sgn_ref, cos_ref, sin_ref):
    ang = pos_ref[...].astype(F32) * inv_ref[...]
    cos_ref[...] = jnp.cos(ang)
    sin_ref[...] = jnp.sin(ang) * sgn_ref[...]


def _rope_tables(positions):
    m = positions.size
    inv_a = ROPE_THETA ** (-jnp.arange(0, HEAD_DIM, 2, dtype=F32) / HEAD_DIM)
    inv_i = ROPE_THETA ** (-jnp.arange(0, IDX_DIM, 2, dtype=F32) / IDX_DIM)
    inv = jnp.concatenate([inv_a, inv_a, inv_i, inv_i, inv_i, inv_i])[None, :]
    ha, hi = HEAD_DIM // 2, IDX_DIM // 2
    sgn = np.concatenate([-np.ones(ha), np.ones(ha),
                          -np.ones(hi), np.ones(hi), -np.ones(hi), np.ones(hi)])
    sgn = jnp.asarray(sgn, F32)[None, :]
    tm = 2048
    return pl.pallas_call(
        _rope_table_kernel,
        grid=(m // tm,),
        in_specs=[pl.BlockSpec((tm, 1), lambda i: (i, 0)),
                  pl.BlockSpec((1, 2 * LANES), lambda i: (0, 0)),
                  pl.BlockSpec((1, 2 * LANES), lambda i: (0, 0))],
        out_specs=[pl.BlockSpec((tm, 2 * LANES), lambda i: (i, 0))] * 2,
        out_shape=[jax.ShapeDtypeStruct((m, 2 * LANES), F32)] * 2,
        compiler_params=_cparams(1),
        name="rope_tables",
    )(positions.reshape(m, 1), inv, sgn)


def _rmsnorm_kernel(x_ref, g_ref, o_ref):
    x = x_ref[...]
    y = x * lax.rsqrt(jnp.mean(x * x, axis=-1, keepdims=True) + NORM_EPS)
    o_ref[...] = (y * g_ref[...]).astype(o_ref.dtype)


def _rmsnorm(x2d, g, out_dtype):
    m, d = x2d.shape
    tm = 1024
    return pl.pallas_call(
        _rmsnorm_kernel,
        grid=(m // tm,),
        in_specs=[pl.BlockSpec((tm, d), lambda i: (i, 0)),
                  pl.BlockSpec((1, d), lambda i: (0, 0))],
        out_specs=pl.BlockSpec((tm, d), lambda i: (i, 0)),
        out_shape=jax.ShapeDtypeStruct((m, d), out_dtype),
        compiler_params=_cparams(1),
        name="rmsnorm",
    )(x2d, g.reshape(1, d))


ROPE_TN = 512


def _rope128(x, cos, sin):
    return x * cos + pltpu.roll(x, HEAD_DIM // 2, 1) * sin


def _rope64(x, cos, sin, first_half):
    half = IDX_DIM // 2
    partner = jnp.where(first_half, pltpu.roll(x, LANES - half, 1),
                        pltpu.roll(x, half, 1))
    return x * cos + partner * sin


def _proj_rope_kernel(h_ref, w_ref, cos_ref, sin_ref, o_ref):
    j = pl.program_id(1)
    acc = jnp.dot(h_ref[...], w_ref[...], preferred_element_type=F32)
    n_sub = ROPE_TN // LANES
    cos_a, sin_a = cos_ref[:, :LANES], sin_ref[:, :LANES]
    cos_i, sin_i = cos_ref[:, LANES:], sin_ref[:, LANES:]
    lane = lax.broadcasted_iota(jnp.int32, (1, LANES), 1)
    first_half = (lane % IDX_DIM) < (IDX_DIM // 2)

    @pl.when(j < 4)
    def _():
        scale = jnp.where(j < 2, HEAD_DIM ** -0.5, 1.0).astype(F32)
        for c in range(n_sub):
            x = acc[:, c * LANES:(c + 1) * LANES]
            o_ref[:, c * LANES:(c + 1) * LANES] = (
                _rope128(x, cos_a, sin_a) * scale).astype(o_ref.dtype)

    @pl.when(jnp.logical_and(j >= 4, j < 6))
    def _():
        for c in range(n_sub):
            x = acc[:, c * LANES:(c + 1) * LANES]
            o_ref[:, c * LANES:(c + 1) * LANES] = _rope64(
                x, cos_i, sin_i, first_half).astype(o_ref.dtype)

    @pl.when(j == 6)
    def _():
        for c in range(n_sub):
            x = acc[:, c * LANES:(c + 1) * LANES]
            if c < 2:
                x = _rope64(x, cos_i, sin_i, first_half)
            o_ref[:, c * LANES:(c + 1) * LANES] = x.astype(o_ref.dtype)


def _proj_rope(h, w, cos, sin):
    m, d = h.shape
    n = w.shape[1]
    tm = 1024
    return pl.pallas_call(
        _proj_rope_kernel,
        grid=(m // tm, n // ROPE_TN),
        in_specs=[pl.BlockSpec((tm, d), lambda i, j: (i, 0)),
                  pl.BlockSpec((d, ROPE_TN), lambda i, j: (0, j)),
                  pl.BlockSpec((tm, 2 * LANES), lambda i, j: (i, 0)),
                  pl.BlockSpec((tm, 2 * LANES), lambda i, j: (i, 0))],
        out_specs=pl.BlockSpec((tm, ROPE_TN), lambda i, j: (i, j)),
        out_shape=jax.ShapeDtypeStruct((m, n), BF16),
        compiler_params=_cparams(2),
        name="proj_rope",
    )(h, w, cos, sin)


def _proj_plain_kernel(h_ref, w_ref, o_ref):
    o_ref[...] = jnp.dot(h_ref[...], w_ref[...],
                         preferred_element_type=F32).astype(o_ref.dtype)


def _proj_plain(h, w, tn, name):
    m, d = h.shape
    n = w.shape[1]
    tm = 1024
    return pl.pallas_call(
        _proj_plain_kernel,
        grid=(m // tm, n // tn),
        in_specs=[pl.BlockSpec((tm, d), lambda i, j: (i, 0)),
                  pl.BlockSpec((d, tn), lambda i, j: (0, j))],
        out_specs=pl.BlockSpec((tm, tn), lambda i, j: (i, j)),
        out_shape=jax.ShapeDtypeStruct((m, n), BF16),
        compiler_params=_cparams(2),
        name=name,
    )(h, w)


ATT_TK = 512


def _proj_vt_kernel(h_ref, wt_ref, o_ref):
    o_ref[...] = lax.dot_general(wt_ref[...], h_ref[...], NT_DIMS,
                                 preferred_element_type=F32).astype(o_ref.dtype)


def _proj_vt(h, wt, b, s):
    d = h.shape[1]
    n = wt.shape[0]
    tk = ATT_TK
    return pl.pallas_call(
        _proj_vt_kernel,
        grid=(b, s // tk),
        in_specs=[pl.BlockSpec((None, tk, d), lambda bi, t: (bi, t, 0)),
                  pl.BlockSpec((n, d), lambda bi, t: (0, 0))],
        out_specs=pl.BlockSpec((None, None, n, tk), lambda bi, t: (bi, t, 0, 0)),
        out_shape=jax.ShapeDtypeStruct((b, s // tk, n, tk), BF16),
        compiler_params=_cparams(2),
        name="proj_vt",
    )(h.reshape(b, s, d), wt)


COARSE_BITS = 16
FINE_UNCHECKED = 4
N_CHAINS = 4
M_INIT = -1e30


def _key16_to_f32(u):
    key = u ^ 0x8000
    bits = jnp.where((key & 0x8000) != 0, key ^ 0x7FFF, key)
    return pltpu.bitcast(jnp.left_shift(bits, 16), F32)


def _f32_to_key32(x):
    bits = pltpu.bitcast(x, jnp.int32)
    return jnp.where(bits < 0, bits ^ 0x7FFFFFFF, bits)


def _key32_to_f32(key):
    return pltpu.bitcast(jnp.where(key < 0, key ^ 0x7FFFFFFF, key), F32)


def _dsa_kernel(q_ref, iq_ref, k_ref, ikw_ref, vt_ref, ga_ref, o_ref,
                sc_ref, sc16_ref, bias_ref, acc_ref, *, topk):
    i = pl.program_id(1)
    tk = ATT_TK
    nk = i // (tk // Q_BLOCK) + 1
    q0 = pl.multiple_of(i * Q_BLOCK, Q_BLOCK)
    qidx = q0 + lax.broadcasted_iota(jnp.int32, (1, Q_BLOCK), 1)

    def tile_start(t):
        return pl.multiple_of(t * tk, tk)

    def key_index(k0):
        return k0 + lax.broadcasted_iota(jnp.int32, (tk, 1), 0)

    iq = iq_ref[...]
    n_pair = IDX_HEADS // 2
    lhs = jnp.concatenate(
        [iq[:, p * LANES:(p + 1) * LANES] for p in range(n_pair)], axis=0)
    w = ikw_ref[pl.ds(q0, Q_BLOCK), 2 * LANES:3 * LANES].astype(F32)
    wt = w.T * ((IDX_HEADS ** -0.5) * (IDX_DIM ** -0.5))
    wrow = [wt[h:h + 1, :] for h in range(IDX_HEADS)]

    def score_tile(t, carry):
        k0 = tile_start(t)
        acc = jnp.zeros((tk, Q_BLOCK), F32)
        for e in range(2):
            ik_e = ikw_ref[pl.ds(k0, tk), e * LANES:(e + 1) * LANES]
            d = lax.dot_general(ik_e, lhs, NT_DIMS,
                                preferred_element_type=F32)
            for p in range(n_pair):
                dp = d[:, p * Q_BLOCK:(p + 1) * Q_BLOCK]
                acc = acc + wrow[2 * p + e] * jnp.maximum(dp, 0.0)
        sc = jnp.where(key_index(k0) <= qidx, acc, NEG_INF)
        sc_ref[pl.ds(k0, tk), :] = sc
        sc16_ref[pl.ds(k0, tk), :] = sc.astype(BF16)
        return carry

    lax.fori_loop(0, nk, score_tile, 0)

    one16 = jnp.ones((), BF16)
    zero16 = jnp.zeros((), BF16)
    n_packed = tk // PACKED_SUBLANES

    def coarse(it, t_u):
        cand_u = t_u | jnp.left_shift(jnp.int32(1), COARSE_BITS - 1 - it)
        cand = _key16_to_f32(cand_u).astype(BF16)

        def count_tile(t, cnts):
            x = sc16_ref[pl.ds(tile_start(t), tk), :]
            ind = jnp.where(x >= cand, one16, zero16)
            cnts = list(cnts)
            for j in range(n_packed):
                cnts[j % N_CHAINS] = cnts[j % N_CHAINS] + ind[
                    j * PACKED_SUBLANES:(j + 1) * PACKED_SUBLANES, :]
            return tuple(cnts)

        cnts = lax.fori_loop(
            0, nk, count_tile,
            (jnp.zeros((PACKED_SUBLANES, Q_BLOCK), BF16),) * N_CHAINS)
        total = jnp.sum(sum(c.astype(F32) for c in cnts), axis=0, keepdims=True)
        return jnp.where(total >= topk, cand_u, t_u)

    t_u = lax.fori_loop(0, COARSE_BITS, coarse,
                        jnp.zeros((1, Q_BLOCK), jnp.int32))

    def count_ge(cand):
        def count_tile(t, cnts):
            s = sc_ref[pl.ds(tile_start(t), tk), :]
            ind = jnp.where(s >= cand, 1.0, 0.0)
            cnts = list(cnts)
            for j in range(tk // SUBLANES):
                cnts[j % N_CHAINS] = cnts[j % N_CHAINS] + ind[
                    j * SUBLANES:(j + 1) * SUBLANES, :]
            return tuple(cnts)

        cnts = lax.fori_loop(
            0, nk, count_tile,
            (jnp.zeros((SUBLANES, Q_BLOCK), F32),) * N_CHAINS)
        return jnp.sum(sum(cnts), axis=0, keepdims=True)

    base = _f32_to_key32(_key16_to_f32(t_u))
    lo0 = base - 0x8000
    hi0 = base + 0x18001

    def fine_cond(st):
        lo, hi, cnt_lo = st
        open_ = jnp.logical_and(cnt_lo != topk, hi - lo > 1)
        return jnp.max(jnp.where(open_, 1, 0)) > 0

    def fine_step(st):
        lo, hi, cnt_lo = st
        mid = lo + jnp.right_shift(hi - lo, 1)
        cnt = count_ge(_key32_to_f32(mid))
        ge = cnt >= topk
        return (jnp.where(ge, mid, lo), jnp.where(ge, hi, mid),
                jnp.where(ge, cnt, cnt_lo))

    st = (lo0, hi0, jnp.full((1, Q_BLOCK), -1.0, F32))
    st = lax.fori_loop(0, FINE_UNCHECKED, lambda _, s_: fine_step(s_), st)
    lo, _, _ = lax.while_loop(fine_cond, lambda s_: fine_step(fine_step(s_)), st)
    thr = jnp.where(qidx < topk, NEG_INF, _key32_to_f32(lo))

    def bias_tile(t, carry):
        k0 = tile_start(t)
        s = sc_ref[pl.ds(k0, tk), :]
        causal_bias = jnp.where(key_index(k0) <= qidx, 0.0, NEG_INF)
        bias_ref[pl.ds(k0, tk), :] = jnp.where(s >= thr, causal_bias, NEG_INF)
        return carry

    lax.fori_loop(0, nk, bias_tile, 0)

    acc_ref[...] = jnp.zeros(acc_ref.shape, F32)

    def att_tile(t, carry):
        ms, ls = carry
        k0 = tile_start(t)
        bias = bias_ref[pl.ds(k0, tk), :]
        new_m, new_l = [], []
        for h in range(N_HEADS):
            hs = slice(h * HEAD_DIM, (h + 1) * HEAD_DIM)
            s = lax.dot_general(k_ref[pl.ds(k0, tk), hs], q_ref[:, hs], NT_DIMS,
                                preferred_element_type=F32) + bias
            m_new = jnp.maximum(ms[h], jnp.max(s, axis=0, keepdims=True))
            alpha = jnp.exp(ms[h] - m_new)
            p = jnp.exp(s - m_new)
            new_l.append(alpha * ls[h] + jnp.sum(
                p.reshape(tk // SUBLANES, SUBLANES, Q_BLOCK), axis=0))
            pv = jnp.dot(vt_ref[t, hs, :], p.astype(BF16),
                         preferred_element_type=F32)
            acc_ref[h] = acc_ref[h] * alpha + pv
            new_m.append(m_new)
        return tuple(new_m), tuple(new_l)

    ms, ls = lax.fori_loop(
        0, nk, att_tile,
        (tuple(jnp.full((1, Q_BLOCK), M_INIT, F32) for _ in range(N_HEADS)),
         tuple(jnp.zeros((SUBLANES, Q_BLOCK), F32) for _ in range(N_HEADS))))

    for h in range(N_HEADS):
        hs = slice(h * HEAD_DIM, (h + 1) * HEAD_DIM)
        out_t = acc_ref[h] / jnp.sum(ls[h], axis=0, keepdims=True)
        gate = ga_ref[:, hs].astype(F32)
        o_ref[:, hs] = (out_t.T * (gate * jax.nn.sigmoid(gate))).astype(o_ref.dtype)


def _dsa_attention(pr, pv, vt, b, s):
    pr3 = pr.reshape(b, s, pr.shape[1])
    pv3 = pv.reshape(b, s, pv.shape[1])
    topk = min(TOPK_MAX, s // 4)
    w_att = ATTN_WIDTH
    single = pl.Buffered(1)
    return pl.pallas_call(
        functools.partial(_dsa_kernel, topk=topk),
        grid=(b, s // Q_BLOCK),
        in_specs=[
            pl.BlockSpec((None, Q_BLOCK, w_att), lambda bi, i: (bi, i, 0)),
            pl.BlockSpec((None, Q_BLOCK, w_att), lambda bi, i: (bi, i, 2)),
            pl.BlockSpec((None, s, w_att), lambda bi, i: (bi, 0, 1),
                         pipeline_mode=single),
            pl.BlockSpec((None, s, 3 * LANES), lambda bi, i: (bi, 0, 8),
                         pipeline_mode=single),
            pl.BlockSpec((None, s // ATT_TK, w_att, ATT_TK),
                         lambda bi, i: (bi, 0, 0, 0), pipeline_mode=single),
            pl.BlockSpec((None, Q_BLOCK, w_att), lambda bi, i: (bi, i, 0)),
        ],
        out_specs=pl.BlockSpec((None, Q_BLOCK, w_att), lambda bi, i: (bi, i, 0)),
        out_shape=jax.ShapeDtypeStruct((b, s, w_att), BF16),
        scratch_shapes=[pltpu.VMEM((s, Q_BLOCK), F32),
                        pltpu.VMEM((s, Q_BLOCK), BF16),
                        pltpu.VMEM((s, Q_BLOCK), F32),
                        pltpu.VMEM((N_HEADS, HEAD_DIM, Q_BLOCK), F32)],
        compiler_params=_cparams(2),
        name="dsa_attention",
    )(pr3, pr3, pr3, pr3, vt, pv3)


RNN_T = 512


def _rglru_kernel(xr_ref, gr_ref, cw_ref, cb_ref, wg_ref, bg_ref, lam_ref,
                  o_ref, xp_ref, a_ref, u_ref, h_ref, hc_ref):
    c = pl.program_id(1)
    t_len = RNN_T
    pad = SUBLANES

    @pl.when(c == 0)
    def _():
        xp_ref[0:pad, :] = jnp.zeros((pad, D_RNN), F32)
        hc_ref[...] = jnp.zeros((1, D_RNN), F32)

    x = xr_ref[...].astype(F32)
    xp_ref[pad:pad + t_len, :] = x
    y = cb_ref[...] + cw_ref[CONV_WIDTH - 1:CONV_WIDTH, :] * x
    for j in range(CONV_WIDTH - 1):
        back = CONV_WIDTH - 1 - j
        y = y + cw_ref[j:j + 1, :] * xp_ref[pad - back:pad - back + t_len, :]
    xp_ref[0:pad, :] = x[t_len - pad:, :]

    g = jnp.dot(y.astype(BF16), wg_ref[...],
                preferred_element_type=F32) + bg_ref[...]
    r = jax.nn.sigmoid(g[:, :D_RNN])
    gi = jax.nn.sigmoid(g[:, D_RNN:])
    z = -lam_ref[...]
    softplus = jnp.maximum(z, 0.0) + jnp.log1p(jnp.exp(-jnp.abs(z)))
    log_a = -LRU_C * r * softplus
    a = jnp.exp(log_a)
    a_ref[...] = a
    u_ref[...] = jnp.sqrt(1.0 - a * a) * (gi * y)

    def step(t, h):
        h = a_ref[pl.ds(t, 1), :] * h + u_ref[pl.ds(t, 1), :]
        h_ref[pl.ds(t, 1), :] = h
        return h

    hc_ref[...] = lax.fori_loop(0, t_len, step, hc_ref[...], unroll=8)
    gr = gr_ref[...].astype(F32)
    o_ref[...] = (h_ref[...] * (gr * jax.nn.sigmoid(gr))).astype(o_ref.dtype)


def _rglru(pc, b, s, conv_w, conv_b, w_gates, b_gates, lam):
    pc3 = pc.reshape(b, s, pc.shape[1])
    t = RNN_T
    const = lambda bi, c: (0, 0)
    return pl.pallas_call(
        _rglru_kernel,
        grid=(b, s // t),
        in_specs=[
            pl.BlockSpec((None, t, D_RNN), lambda bi, c: (bi, c, 0)),
            pl.BlockSpec((None, t, D_RNN), lambda bi, c: (bi, c, 1)),
            pl.BlockSpec((CONV_WIDTH, D_RNN), const),
            pl.BlockSpec((1, D_RNN), const),
            pl.BlockSpec((D_RNN, 2 * D_RNN), const),
            pl.BlockSpec((1, 2 * D_RNN), const),
            pl.BlockSpec((1, D_RNN), const),
        ],
        out_specs=pl.BlockSpec((None, t, D_RNN), lambda bi, c: (bi, c, 0)),
        out_shape=jax.ShapeDtypeStruct((b, s, D_RNN), BF16),
        scratch_shapes=[pltpu.VMEM((t + SUBLANES, D_RNN), F32),
                        pltpu.VMEM((t, D_RNN), F32),
                        pltpu.VMEM((t, D_RNN), F32),
                        pltpu.VMEM((t, D_RNN), F32),
                        pltpu.VMEM((1, D_RNN), F32)],
        compiler_params=_cparams(2),
        name="rglru",
    )(pc3, pc3, conv_w, conv_b.reshape(1, D_RNN), w_gates,
      b_gates.reshape(1, 2 * D_RNN), lam.reshape(1, D_RNN))


def _merge_kernel(at_ref, hr_ref, ma_ref, mb_ref, x_ref, woa_ref, wor_ref,
                  wo_ref, g_ref, xo_ref, no_ref):
    y_a = jnp.dot(at_ref[...], woa_ref[...], preferred_element_type=F32)
    y_b = jnp.dot(hr_ref[...], wor_ref[...], preferred_element_type=F32)
    merged = (jax.nn.sigmoid(ma_ref[...].astype(F32)) * y_a
              + jax.nn.sigmoid(mb_ref[...].astype(F32)) * y_b)
    x = x_ref[...] + jnp.dot(merged.astype(BF16), wo_ref[...],
                             preferred_element_type=F32)
    xo_ref[...] = x
    y = x * lax.rsqrt(jnp.mean(x * x, axis=-1, keepdims=True) + NORM_EPS)
    no_ref[...] = (y * g_ref[...]).astype(no_ref.dtype)


def _merge(attn_g, hr_g, pv, x2d, w_oa, w_or, w_o, g, norm_dtype):
    m, d = x2d.shape
    tm = 512
    const = lambda i: (0, 0)
    return pl.pallas_call(
        _merge_kernel,
        grid=(m // tm,),
        in_specs=[
            pl.BlockSpec((tm, ATTN_WIDTH), lambda i: (i, 0)),
            pl.BlockSpec((tm, D_RNN), lambda i: (i, 0)),
            pl.BlockSpec((tm, d), lambda i: (i, 1)),
            pl.BlockSpec((tm, d), lambda i: (i, 2)),
            pl.BlockSpec((tm, d), lambda i: (i, 0)),
            pl.BlockSpec((ATTN_WIDTH, d), const),
            pl.BlockSpec((D_RNN, d), const),
            pl.BlockSpec((d, d), const),
            pl.BlockSpec((1, d), const),
        ],
        out_specs=[pl.BlockSpec((tm, d), lambda i: (i, 0))] * 2,
        out_shape=[jax.ShapeDtypeStruct((m, d), F32),
                   jax.ShapeDtypeStruct((m, d), norm_dtype)],
        compiler_params=_cparams(1),
        name="merge",
    )(attn_g, hr_g, pv, pv, x2d, w_oa, w_or, w_o, g.reshape(1, d))


def _pack_weights(w_in_l, w_rg_l, w_ig_l):
    offs = np.cumsum((ATTN_WIDTH,) * 4 + (IDX_HEADS * IDX_DIM, IDX_DIM, IDX_HEADS,
                                          D_RNN, D_RNN, D_MODEL, D_MODEL))[:-1]
    q, k, v, ga, iq, ik, iw, xr, gr, ma, mb = jnp.split(w_in_l, offs.tolist(), axis=1)
    d = w_in_l.shape[0]
    z = lambda n: jnp.zeros((d, n), w_in_l.dtype)
    tail = jnp.concatenate([ik, z(LANES - IDX_DIM), z(LANES - IDX_DIM), ik,
                            iw, z(ROPE_TN - 2 * LANES - IDX_HEADS)], axis=1)
    w_rope = jnp.concatenate([q, k, iq, tail], axis=1).astype(BF16)
    w_gate = jnp.concatenate([ga, ma, mb], axis=1).astype(BF16)
    w_vt = v.T.astype(BF16)
    w_rnn = jnp.concatenate([xr, gr], axis=1).astype(BF16)
    w_gates = jnp.concatenate([jax.scipy.linalg.block_diag(*w_rg_l),
                               jax.scipy.linalg.block_diag(*w_ig_l)],
                              axis=1).astype(BF16)
    return w_rope, w_gate, w_vt, w_rnn, w_gates


def kernel(x, positions, norm_g, w_in, conv_w, conv_b, w_rg, b_rg, w_ig, b_ig,
           lru_lambda, w_out_attn, w_out_rnn, w_o, final_g):
    b, s, d = x.shape
    depth = w_in.shape[0]
    m = b * s
    cos, sin = _rope_tables(positions)
    x2d = x.reshape(m, d)
    h = _rmsnorm(x2d, norm_g[0], BF16)
    out = None
    for l in range(depth):
        w_rope, w_gate, w_vt, w_rnn, w_gates = _pack_weights(
            w_in[l], w_rg[l], w_ig[l])
        pr = _proj_rope(h, w_rope, cos, sin)
        pv = _proj_plain(h, w_gate, 1024, "proj_gate")
        vt = _proj_vt(h, w_vt, b, s)
        pc = _proj_plain(h, w_rnn, D_RNN, "proj_rnn")
        attn_g = _dsa_attention(pr, pv, vt, b, s).reshape(m, ATTN_WIDTH)
        hr_g = _rglru(pc, b, s, conv_w[l], conv_b[l], w_gates,
                      jnp.concatenate([b_rg[l], b_ig[l]]), lru_lambda[l]
                      ).reshape(m, D_RNN)
        last = l == depth - 1
        g_next = final_g if last else norm_g[l + 1]
        x2d, normed = _merge(attn_g, hr_g, pv, x2d,
                             w_out_attn[l].astype(BF16), w_out_rnn[l].astype(BF16),
                             w_o[l].astype(BF16), g_next, F32 if last else BF16)
        if last:
            out = normed
        else:
            h = normed
    return out.reshape(b, s, d)
```

```python
import functools

import jax
import jax.numpy as jnp
import numpy as np
from jax import lax
from jax.experimental import pallas as pl
from jax.experimental.pallas import tpu as pltpu

D_MODEL = 1024
N_HEADS = 8
HEAD_DIM = 128
ATTN_WIDTH = N_HEADS * HEAD_DIM
IDX_HEADS = 16
IDX_DIM = 64
TOPK_MAX = 256
Q_BLOCK = 128
D_RNN = 1408
RNN_BLOCKS = 16
RNN_BLOCK_DIM = D_RNN // RNN_BLOCKS
CONV_WIDTH = 4
LRU_C = 8.0
ROPE_THETA = 10000.0
NORM_EPS = 1e-6

LANES = 128
SUBLANES = 8
PACKED_SUBLANES = 16
VMEM_LIMIT = 56 * 1024 * 1024

F32 = jnp.float32
BF16 = jnp.bfloat16
NEG_INF = float("-inf")
NT_DIMS = (((1,), (1,)), ((), ()))
LOG2E = float(np.log2(np.e))
V_ROWS = HEAD_DIM + SUBLANES


def _sigmoid(x):
    return 0.5 * jnp.tanh(0.5 * x) + 0.5


def _cparams(n_grid):
    return pltpu.CompilerParams(
        dimension_semantics=("arbitrary",) * n_grid,
        vmem_limit_bytes=VMEM_LIMIT)


def _rope_table_kernel(pos_ref, inv_ref, sgn_ref, cos_ref, sin_ref):
    ang = pos_ref[...].astype(F32) * inv_ref[...]
    cos_ref[...] = jnp.cos(ang)
    sin_ref[...] = jnp.sin(ang) * sgn_ref[...]


def _rope_tables(positions):
    m = positions.size
    inv_a = ROPE_THETA ** (-jnp.arange(0, HEAD_DIM, 2, dtype=F32) / HEAD_DIM)
    inv_i = ROPE_THETA ** (-jnp.arange(0, IDX_DIM, 2, dtype=F32) / IDX_DIM)
    inv = jnp.concatenate([inv_a, inv_a, inv_i, inv_i, inv_i, inv_i])[None, :]
    ha, hi = HEAD_DIM // 2, IDX_DIM // 2
    sgn = np.concatenate([-np.ones(ha), np.ones(ha),
                          -np.ones(hi), np.ones(hi), -np.ones(hi), np.ones(hi)])
    sgn = jnp.asarray(sgn, F32)[None, :]
    tm = 2048
    return pl.pallas_call(
        _rope_table_kernel,
        grid=(m // tm,),
        in_specs=[pl.BlockSpec((tm, 1), lambda i: (i, 0)),
                  pl.BlockSpec((1, 2 * LANES), lambda i: (0, 0)),
                  pl.BlockSpec((1, 2 * LANES), lambda i: (0, 0))],
        out_specs=[pl.BlockSpec((tm, 2 * LANES), lambda i: (i, 0))] * 2,
        out_shape=[jax.ShapeDtypeStruct((m, 2 * LANES), F32)] * 2,
        compiler_params=_cparams(1),
        name="rope_tables",
    )(positions.reshape(m, 1), inv, sgn)


def _rmsnorm_kernel(x_ref, g_ref, o_ref):
    x = x_ref[...]
    y = x * lax.rsqrt(jnp.mean(x * x, axis=-1, keepdims=True) + NORM_EPS)
    o_ref[...] = (y * g_ref[...]).astype(o_ref.dtype)


def _rmsnorm(x2d, g, out_dtype):
    m, d = x2d.shape
    tm = 1024
    return pl.pallas_call(
        _rmsnorm_kernel,
        grid=(m // tm,),
        in_specs=[pl.BlockSpec((tm, d), lambda i: (i, 0)),
                  pl.BlockSpec((1, d), lambda i: (0, 0))],
        out_specs=pl.BlockSpec((tm, d), lambda i: (i, 0)),
        out_shape=jax.ShapeDtypeStruct((m, d), out_dtype),
        compiler_params=_cparams(1),
        name="rmsnorm",
    )(x2d, g.reshape(1, d))


ROPE_TN = 512


def _rope128(x, cos, sin):
    return x * cos + pltpu.roll(x, HEAD_DIM // 2, 1) * sin


def _rope64(x, cos, sin, first_half):
    half = IDX_DIM // 2
    partner = jnp.where(first_half, pltpu.roll(x, LANES - half, 1),
                        pltpu.roll(x, half, 1))
    return x * cos + partner * sin


def _proj_rope_kernel(h_ref, w_ref, cos_ref, sin_ref, o_ref):
    j = pl.program_id(1)
    acc = jnp.dot(h_ref[...], w_ref[...], preferred_element_type=F32)
    n_sub = ROPE_TN // LANES
    cos_a, sin_a = cos_ref[:, :LANES], sin_ref[:, :LANES]
    cos_i, sin_i = cos_ref[:, LANES:], sin_ref[:, LANES:]
    lane = lax.broadcasted_iota(jnp.int32, (1, LANES), 1)
    first_half = (lane % IDX_DIM) < (IDX_DIM // 2)

    @pl.when(j < 4)
    def _():
        scale = jnp.where(j < 2, (HEAD_DIM ** -0.5) * LOG2E, 1.0).astype(F32)
        for c in range(n_sub):
            x = acc[:, c * LANES:(c + 1) * LANES]
            o_ref[:, c * LANES:(c + 1) * LANES] = (
                _rope128(x, cos_a, sin_a) * scale).astype(o_ref.dtype)

    @pl.when(jnp.logical_and(j >= 4, j < 6))
    def _():
        for c in range(n_sub):
            x = acc[:, c * LANES:(c + 1) * LANES]
            o_ref[:, c * LANES:(c + 1) * LANES] = _rope64(
                x, cos_i, sin_i, first_half).astype(o_ref.dtype)

    @pl.when(j == 6)
    def _():
        for c in range(n_sub):
            x = acc[:, c * LANES:(c + 1) * LANES]
            if c < 2:
                x = _rope64(x, cos_i, sin_i, first_half)
            o_ref[:, c * LANES:(c + 1) * LANES] = x.astype(o_ref.dtype)


def _proj_rope(h, w, cos, sin):
    m, d = h.shape
    n = w.shape[1]
    tm = 1024
    return pl.pallas_call(
        _proj_rope_kernel,
        grid=(m // tm, n // ROPE_TN),
        in_specs=[pl.BlockSpec((tm, d), lambda i, j: (i, 0)),
                  pl.BlockSpec((d, ROPE_TN), lambda i, j: (0, j)),
                  pl.BlockSpec((tm, 2 * LANES), lambda i, j: (i, 0)),
                  pl.BlockSpec((tm, 2 * LANES), lambda i, j: (i, 0))],
        out_specs=pl.BlockSpec((tm, ROPE_TN), lambda i, j: (i, j)),
        out_shape=jax.ShapeDtypeStruct((m, n), BF16),
        compiler_params=_cparams(2),
        name="proj_rope",
    )(h, w, cos, sin)


def _proj_plain_kernel(h_ref, w_ref, o_ref):
    o_ref[...] = jnp.dot(h_ref[...], w_ref[...],
                         preferred_element_type=F32).astype(o_ref.dtype)


def _proj_plain(h, w, tn, name):
    m, d = h.shape
    n = w.shape[1]
    tm = 1024
    return pl.pallas_call(
        _proj_plain_kernel,
        grid=(m // tm, n // tn),
        in_specs=[pl.BlockSpec((tm, d), lambda i, j: (i, 0)),
                  pl.BlockSpec((d, tn), lambda i, j: (0, j))],
        out_specs=pl.BlockSpec((tm, tn), lambda i, j: (i, j)),
        out_shape=jax.ShapeDtypeStruct((m, n), BF16),
        compiler_params=_cparams(2),
        name=name,
    )(h, w)


ATT_TK = 512


def _proj_vt_kernel(h_ref, wt_ref, o_ref):
    vt = lax.dot_general(wt_ref[...], h_ref[...], NT_DIMS,
                         preferred_element_type=F32)
    row = lax.broadcasted_iota(jnp.int32, (vt.shape[0], 1), 0)
    o_ref[...] = jnp.where(row % V_ROWS == HEAD_DIM, 1.0, vt).astype(o_ref.dtype)


def _proj_vt(h, wt, b, s):
    d = h.shape[1]
    n = wt.shape[0]
    tk = ATT_TK
    return pl.pallas_call(
        _proj_vt_kernel,
        grid=(b, s // tk),
        in_specs=[pl.BlockSpec((None, tk, d), lambda bi, t: (bi, t, 0)),
                  pl.BlockSpec((n, d), lambda bi, t: (0, 0))],
        out_specs=pl.BlockSpec((None, None, n, tk), lambda bi, t: (bi, t, 0, 0)),
        out_shape=jax.ShapeDtypeStruct((b, s // tk, n, tk), BF16),
        compiler_params=_cparams(2),
        name="proj_vt",
    )(h.reshape(b, s, d), wt)


COARSE_BITS = 16
FINE_UNCHECKED = 8
N_CHAINS = 4
M_INIT = -1e30


def _key16_to_f32(u):
    key = u ^ 0x8000
    bits = jnp.where((key & 0x8000) != 0, key ^ 0x7FFF, key)
    return pltpu.bitcast(jnp.left_shift(bits, 16), F32)


def _f32_to_key32(x):
    bits = pltpu.bitcast(x, jnp.int32)
    return jnp.where(bits < 0, bits ^ 0x7FFFFFFF, bits)


def _key32_to_f32(key):
    return pltpu.bitcast(jnp.where(key < 0, key ^ 0x7FFFFFFF, key), F32)


def _dsa_kernel(q_ref, iq_ref, k_ref, ikw_ref, vt_ref, ga_ref, o_ref,
                sc_ref, sc16_ref, bias_ref, acc_ref, p_ref, *, topk):
    i = pl.program_id(1)
    tk = ATT_TK
    nk = i // (tk // Q_BLOCK) + 1
    q0 = pl.multiple_of(i * Q_BLOCK, Q_BLOCK)
    qidx = q0 + lax.broadcasted_iota(jnp.int32, (1, Q_BLOCK), 1)

    def tile_start(t):
        return pl.multiple_of(t * tk, tk)

    def key_index(k0):
        return k0 + lax.broadcasted_iota(jnp.int32, (tk, 1), 0)

    iq = iq_ref[...]
    n_pair = IDX_HEADS // 2
    lhs = jnp.concatenate(
        [iq[:, p * LANES:(p + 1) * LANES] for p in range(n_pair)], axis=0)
    w = ikw_ref[pl.ds(q0, Q_BLOCK), 2 * LANES:3 * LANES].astype(F32)
    wt = w.T * ((IDX_HEADS ** -0.5) * (IDX_DIM ** -0.5))
    wrow = [wt[h:h + 1, :] for h in range(IDX_HEADS)]

    def score_tile(t, carry):
        k0 = tile_start(t)
        acc = jnp.zeros((tk, Q_BLOCK), F32)
        for e in range(2):
            ik_e = ikw_ref[pl.ds(k0, tk), e * LANES:(e + 1) * LANES]
            d = lax.dot_general(ik_e, lhs, NT_DIMS,
                                preferred_element_type=F32)
            for p in range(n_pair):
                dp = d[:, p * Q_BLOCK:(p + 1) * Q_BLOCK]
                acc = acc + wrow[2 * p + e] * jnp.maximum(dp, 0.0)
        sc = jnp.where(key_index(k0) <= qidx, acc, NEG_INF)
        sc_ref[pl.ds(k0, tk), :] = sc
        sc16_ref[pl.ds(k0, tk), :] = sc.astype(BF16)
        return carry

    lax.fori_loop(0, nk, score_tile, 0)

    one16 = jnp.ones((), BF16)
    zero16 = jnp.zeros((), BF16)
    n_packed = tk // PACKED_SUBLANES

    def coarse(it, t_u):
        cand_u = t_u | jnp.left_shift(jnp.int32(1), COARSE_BITS - 1 - it)
        cand = _key16_to_f32(cand_u).astype(BF16)

        def count_tile(t, cnts):
            x = sc16_ref[pl.ds(tile_start(t), tk), :]
            ind = jnp.where(x >= cand, one16, zero16)
            cnts = list(cnts)
            for j in range(n_packed):
                cnts[j % N_CHAINS] = cnts[j % N_CHAINS] + ind[
                    j * PACKED_SUBLANES:(j + 1) * PACKED_SUBLANES, :]
            return tuple(cnts)

        cnts = lax.fori_loop(
            0, nk, count_tile,
            (jnp.zeros((PACKED_SUBLANES, Q_BLOCK), BF16),) * N_CHAINS)
        total = jnp.sum(sum(c.astype(F32) for c in cnts), axis=0, keepdims=True)
        return jnp.where(total >= topk, cand_u, t_u)

    t_u = lax.fori_loop(0, COARSE_BITS, coarse,
                        jnp.zeros((1, Q_BLOCK), jnp.int32))

    def count_ge(cand):
        def count_tile(t, cnts):
            s = sc_ref[pl.ds(tile_start(t), tk), :]
            ind = jnp.where(s >= cand, 1.0, 0.0)
            cnts = list(cnts)
            for j in range(tk // SUBLANES):
                cnts[j % N_CHAINS] = cnts[j % N_CHAINS] + ind[
                    j * SUBLANES:(j + 1) * SUBLANES, :]
            return tuple(cnts)

        cnts = lax.fori_loop(
            0, nk, count_tile,
            (jnp.zeros((SUBLANES, Q_BLOCK), F32),) * N_CHAINS)
        return jnp.sum(sum(cnts), axis=0, keepdims=True)

    base = _f32_to_key32(_key16_to_f32(t_u))
    lo0 = base - 0x8000
    hi0 = base + 0x18001

    def fine_cond(st):
        lo, hi, cnt_lo = st
        open_ = jnp.logical_and(cnt_lo != topk, hi - lo > 1)
        return jnp.max(jnp.where(open_, 1.0, 0.0)) > 0.0

    def fine_step(st):
        lo, hi, cnt_lo = st
        mid = lo + jnp.right_shift(hi - lo, 1)
        cnt = count_ge(_key32_to_f32(mid))
        ge = cnt >= topk
        return (jnp.where(ge, mid, lo), jnp.where(ge, hi, mid),
                jnp.where(ge, cnt, cnt_lo))

    st = (lo0, hi0, jnp.full((1, Q_BLOCK), -1.0, F32))
    st = lax.fori_loop(0, FINE_UNCHECKED, lambda _, s_: fine_step(s_), st)
    lo, _, _ = lax.while_loop(fine_cond, lambda s_: fine_step(fine_step(s_)), st)
    thr = jnp.where(qidx < topk, NEG_INF, _key32_to_f32(lo))

    def bias_tile(t, carry):
        k0 = tile_start(t)
        s = sc_ref[pl.ds(k0, tk), :]
        causal_bias = jnp.where(key_index(k0) <= qidx, 0.0, NEG_INF)
        bias_ref[pl.ds(k0, tk), :] = jnp.where(s >= thr, causal_bias, NEG_INF)
        return carry

    lax.fori_loop(0, nk, bias_tile, 0)

    acc_ref[...] = jnp.zeros(acc_ref.shape, F32)

    def qk_head(t, h, m_old):
        k0 = tile_start(t)
        hs = slice(h * HEAD_DIM, (h + 1) * HEAD_DIM)
        s = lax.dot_general(k_ref[pl.ds(k0, tk), hs], q_ref[:, hs], NT_DIMS,
                            preferred_element_type=F32)
        s = s + bias_ref[pl.ds(k0, tk), :]
        m_new = jnp.maximum(m_old, jnp.max(s, axis=0, keepdims=True))
        p_ref[t % 2, h] = jnp.exp2(s - m_new).astype(BF16)
        return m_new, jnp.exp2(m_old - m_new)

    def pv_head(t, h, alpha):
        pv = jnp.dot(vt_ref[t, h * V_ROWS:(h + 1) * V_ROWS, :], p_ref[t % 2, h],
                     preferred_element_type=F32)
        acc_ref[h] = acc_ref[h] * alpha + pv

    def qk_stage(t, ms):
        pairs = [qk_head(t, h, ms[h]) for h in range(N_HEADS)]
        return tuple(p[0] for p in pairs), tuple(p[1] for p in pairs)

    def pv_stage(t, alphas):
        for h in range(N_HEADS):
            pv_head(t, h, alphas[h])

    def att_tile(t, carry):
        ms, alphas = carry
        pairs = []
        for h in range(N_HEADS):
            pv_head(t - 1, h, alphas[h])
            pairs.append(qk_head(t, h, ms[h]))
        return tuple(p[0] for p in pairs), tuple(p[1] for p in pairs)

    m0 = tuple(jnp.full((1, Q_BLOCK), M_INIT, F32) for _ in range(N_HEADS))
    _, alphas = lax.fori_loop(1, nk, att_tile, qk_stage(0, m0))
    pv_stage(nk - 1, alphas)

    for h in range(N_HEADS):
        hs = slice(h * HEAD_DIM, (h + 1) * HEAD_DIM)
        out_t = acc_ref[h, :HEAD_DIM, :] / acc_ref[h, HEAD_DIM:HEAD_DIM + 1, :]
        gate = ga_ref[:, hs].astype(F32)
        o_ref[:, hs] = (out_t.T * (gate * _sigmoid(gate))).astype(o_ref.dtype)


def _dsa_attention(pr, pv, vt, b, s):
    pr3 = pr.reshape(b, s, pr.shape[1])
    pv3 = pv.reshape(b, s, pv.shape[1])
    topk = min(TOPK_MAX, s // 4)
    w_att = ATTN_WIDTH
    single = pl.Buffered(1)
    return pl.pallas_call(
        functools.partial(_dsa_kernel, topk=topk),
        grid=(b, s // Q_BLOCK),
        in_specs=[
            pl.BlockSpec((None, Q_BLOCK, w_att), lambda bi, i: (bi, i, 0)),
            pl.BlockSpec((None, Q_BLOCK, w_att), lambda bi, i: (bi, i, 2)),
            pl.BlockSpec((None, s, w_att), lambda bi, i: (bi, 0, 1),
                         pipeline_mode=single),
            pl.BlockSpec((None, s, 3 * LANES), lambda bi, i: (bi, 0, 8),
                         pipeline_mode=single),
            pl.BlockSpec((None, s // ATT_TK, N_HEADS * V_ROWS, ATT_TK),
                         lambda bi, i: (bi, 0, 0, 0), pipeline_mode=single),
            pl.BlockSpec((None, Q_BLOCK, w_att), lambda bi, i: (bi, i, 0)),
        ],
        out_specs=pl.BlockSpec((None, Q_BLOCK, w_att), lambda bi, i: (bi, i, 0)),
        out_shape=jax.ShapeDtypeStruct((b, s, w_att), BF16),
        scratch_shapes=[pltpu.VMEM((s, Q_BLOCK), F32),
                        pltpu.VMEM((s, Q_BLOCK), BF16),
                        pltpu.VMEM((s, Q_BLOCK), F32),
                        pltpu.VMEM((N_HEADS, V_ROWS, Q_BLOCK), F32),
                        pltpu.VMEM((2, N_HEADS, ATT_TK, Q_BLOCK), BF16)],
        compiler_params=_cparams(2),
        name="dsa_attention",
    )(pr3, pr3, pr3, pr3, vt, pv3)


RNN_T = 512
N_RNN_TILES = D_RNN // LANES
GATE_WIN = 3


def _gate_window_start(j):
    assert RNN_BLOCK_DIM <= LANES
    return min(max(j - 1, 0), N_RNN_TILES - GATE_WIN)


def _rglru_kernel(xr_ref, gr_ref, cw_ref, cb_ref, wg_ref, bg_ref, lam_ref,
                  o_ref, xp_ref, a_ref, u_ref, h_ref, hc_ref):
    c = pl.program_id(1)
    t_len = RNN_T
    pad = SUBLANES

    @pl.when(c == 0)
    def _():
        xp_ref[0:pad, :] = jnp.zeros((pad, D_RNN), F32)
        hc_ref[...] = jnp.zeros((1, D_RNN), F32)

    x = xr_ref[...].astype(F32)
    xp_ref[pad:pad + t_len, :] = x
    y = cb_ref[...] + cw_ref[CONV_WIDTH - 1:CONV_WIDTH, :] * x
    for j in range(CONV_WIDTH - 1):
        back = CONV_WIDTH - 1 - j
        y = y + cw_ref[j:j + 1, :] * xp_ref[pad - back:pad - back + t_len, :]
    xp_ref[0:pad, :] = x[t_len - pad:, :]

    y16 = y.astype(BF16)
    yh = 0.5 * y
    z = -lam_ref[...]
    softplus = jnp.maximum(z, 0.0) + jnp.log1p(jnp.exp(-jnp.abs(z)))
    k = (-0.5 * LRU_C * LOG2E) * softplus
    for j in range(N_RNN_TILES):
        ls = slice(j * LANES, (j + 1) * LANES)
        w0 = _gate_window_start(j) * LANES
        g = jnp.dot(y16[:, w0:w0 + GATE_WIN * LANES], wg_ref[j],
                    preferred_element_type=F32) + bg_ref[j]
        t_r = jnp.tanh(g[:, :LANES])
        t_i = jnp.tanh(g[:, LANES:])
        a = jnp.exp2(k[:, ls] * t_r + k[:, ls])
        a_ref[:, ls] = a
        u_ref[:, ls] = jnp.sqrt(1.0 - a * a) * (t_i * yh[:, ls] + yh[:, ls])

    def step(t, h):
        h = a_ref[pl.ds(t, 1), :] * h + u_ref[pl.ds(t, 1), :]
        h_ref[pl.ds(t, 1), :] = h
        return h

    hc_ref[...] = lax.fori_loop(0, t_len, step, hc_ref[...], unroll=8)
    gh = 0.5 * gr_ref[...].astype(F32)
    o_ref[...] = (h_ref[...] * (gh * jnp.tanh(gh) + gh)).astype(o_ref.dtype)


def _rglru(pc, b, s, conv_w, conv_b, w_gates, b_gates, lam):
    pc3 = pc.reshape(b, s, pc.shape[1])
    t = RNN_T
    const = lambda bi, c: (0, 0)
    const3 = lambda bi, c: (0, 0, 0)
    return pl.pallas_call(
        _rglru_kernel,
        grid=(b, s // t),
        in_specs=[
            pl.BlockSpec((None, t, D_RNN), lambda bi, c: (bi, c, 0)),
            pl.BlockSpec((None, t, D_RNN), lambda bi, c: (bi, c, 1)),
            pl.BlockSpec((CONV_WIDTH, D_RNN), const),
            pl.BlockSpec((1, D_RNN), const),
            pl.BlockSpec((N_RNN_TILES, GATE_WIN * LANES, 2 * LANES), const3),
            pl.BlockSpec((N_RNN_TILES, 1, 2 * LANES), const3),
            pl.BlockSpec((1, D_RNN), const),
        ],
        out_specs=pl.BlockSpec((None, t, D_RNN), lambda bi, c: (bi, c, 0)),
        out_shape=jax.ShapeDtypeStruct((b, s, D_RNN), BF16),
        scratch_shapes=[pltpu.VMEM((t + SUBLANES, D_RNN), F32),
                        pltpu.VMEM((t, D_RNN), F32),
                        pltpu.VMEM((t, D_RNN), F32),
                        pltpu.VMEM((t, D_RNN), F32),
                        pltpu.VMEM((1, D_RNN), F32)],
        compiler_params=_cparams(2),
        name="rglru",
    )(pc3, pc3, conv_w, conv_b.reshape(1, D_RNN), w_gates, b_gates,
      lam.reshape(1, D_RNN))


def _pack_gates(w_rg_l, b_rg_l, w_ig_l, b_ig_l):
    dense = [jax.scipy.linalg.block_diag(*w) for w in (w_rg_l, w_ig_l)]
    bands, biases = [], []
    for j in range(N_RNN_TILES):
        w0 = _gate_window_start(j) * LANES
        cols = slice(j * LANES, (j + 1) * LANES)
        bands.append(jnp.concatenate(
            [dm[w0:w0 + GATE_WIN * LANES, cols] for dm in dense], axis=1))
        biases.append(jnp.concatenate([b_rg_l[cols], b_ig_l[cols]])[None, :])
    return (0.5 * jnp.stack(bands)).astype(BF16), 0.5 * jnp.stack(biases)


def _merge_kernel(at_ref, hr_ref, ma_ref, mb_ref, x_ref, woa_ref, wor_ref,
                  wo_ref, g_ref, xo_ref, no_ref):
    y_a = jnp.dot(at_ref[...], woa_ref[...], preferred_element_type=F32)
    y_b = jnp.dot(hr_ref[...], wor_ref[...], preferred_element_type=F32)
    merged = (_sigmoid(ma_ref[...].astype(F32)) * y_a
              + _sigmoid(mb_ref[...].astype(F32)) * y_b)
    x = x_ref[...] + jnp.dot(merged.astype(BF16), wo_ref[...],
                             preferred_element_type=F32)
    xo_ref[...] = x
    y = x * lax.rsqrt(jnp.mean(x * x, axis=-1, keepdims=True) + NORM_EPS)
    no_ref[...] = (y * g_ref[...]).astype(no_ref.dtype)


def _merge(attn_g, hr_g, pv, x2d, w_oa, w_or, w_o, g, norm_dtype):
    m, d = x2d.shape
    tm = 512
    const = lambda i: (0, 0)
    return pl.pallas_call(
        _merge_kernel,
        grid=(m // tm,),
        in_specs=[
            pl.BlockSpec((tm, ATTN_WIDTH), lambda i: (i, 0)),
            pl.BlockSpec((tm, D_RNN), lambda i: (i, 0)),
            pl.BlockSpec((tm, d), lambda i: (i, 1)),
            pl.BlockSpec((tm, d), lambda i: (i, 2)),
            pl.BlockSpec((tm, d), lambda i: (i, 0)),
            pl.BlockSpec((ATTN_WIDTH, d), const),
            pl.BlockSpec((D_RNN, d), const),
            pl.BlockSpec((d, d), const),
            pl.BlockSpec((1, d), const),
        ],
        out_specs=[pl.BlockSpec((tm, d), lambda i: (i, 0))] * 2,
        out_shape=[jax.ShapeDtypeStruct((m, d), F32),
                   jax.ShapeDtypeStruct((m, d), norm_dtype)],
        compiler_params=_cparams(1),
        name="merge",
    )(attn_g, hr_g, pv, pv, x2d, w_oa, w_or, w_o, g.reshape(1, d))


def _pack_weights(w_in_l):
    offs = np.cumsum((ATTN_WIDTH,) * 4 + (IDX_HEADS * IDX_DIM, IDX_DIM, IDX_HEADS,
                                          D_RNN, D_RNN, D_MODEL, D_MODEL))[:-1]
    q, k, v, ga, iq, ik, iw, xr, gr, ma, mb = jnp.split(w_in_l, offs.tolist(), axis=1)
    d = w_in_l.shape[0]
    z = lambda n: jnp.zeros((d, n), w_in_l.dtype)
    tail = jnp.concatenate([ik, z(LANES - IDX_DIM), z(LANES - IDX_DIM), ik,
                            iw, z(ROPE_TN - 2 * LANES - IDX_HEADS)], axis=1)
    w_rope = jnp.concatenate([q, k, iq, tail], axis=1).astype(BF16)
    w_gate = jnp.concatenate([ga, ma, mb], axis=1).astype(BF16)
    vt = v.T.reshape(N_HEADS, HEAD_DIM, d)
    w_vt = jnp.pad(vt, ((0, 0), (0, V_ROWS - HEAD_DIM), (0, 0))).reshape(
        N_HEADS * V_ROWS, d).astype(BF16)
    w_rnn = jnp.concatenate([xr, gr], axis=1).astype(BF16)
    return w_rope, w_gate, w_vt, w_rnn


def kernel(x, positions, norm_g, w_in, conv_w, conv_b, w_rg, b_rg, w_ig, b_ig,
           lru_lambda, w_out_attn, w_out_rnn, w_o, final_g):
    b, s, d = x.shape
    depth = w_in.shape[0]
    m = b * s
    cos, sin = _rope_tables(positions)
    x2d = x.reshape(m, d)
    h = _rmsnorm(x2d, norm_g[0], BF16)
    out = None
    for l in range(depth):
        w_rope, w_gate, w_vt, w_rnn = _pack_weights(w_in[l])
        w_gates, b_gates = _pack_gates(w_rg[l], b_rg[l], w_ig[l], b_ig[l])
        pr = _proj_rope(h, w_rope, cos, sin)
        pv = _proj_plain(h, w_gate, 1024, "proj_gate")
        vt = _proj_vt(h, w_vt, b, s)
        pc = _proj_plain(h, w_rnn, D_RNN, "proj_rnn")
        attn_g = _dsa_attention(pr, pv, vt, b, s).reshape(m, ATTN_WIDTH)
        hr_g = _rglru(pc, b, s, conv_w[l], conv_b[l], w_gates, b_gates,
                      lru_lambda[l]).reshape(m, D_RNN)
        last = l == depth - 1
        g_next = final_g if last else norm_g[l + 1]
        x2d, normed = _merge(attn_g, hr_g, pv, x2d,
                             w_out_attn[l].astype(BF16), w_out_rnn[l].astype(BF16),
                             w_o[l].astype(BF16), g_next, F32 if last else BF16)
        if last:
            out = normed
        else:
            h = normed
    return out.reshape(b, s, d)
```

```python
import functools

import jax
import jax.numpy as jnp
import numpy as np
from jax import lax
from jax.experimental import pallas as pl
from jax.experimental.pallas import tpu as pltpu

D_MODEL = 1024
N_HEADS = 8
HEAD_DIM = 128
ATTN_WIDTH = N_HEADS * HEAD_DIM
IDX_HEADS = 16
IDX_DIM = 64
TOPK_MAX = 256
Q_BLOCK = 256
D_RNN = 1408
RNN_BLOCKS = 16
RNN_BLOCK_DIM = D_RNN // RNN_BLOCKS
CONV_WIDTH = 4
LRU_C = 8.0
ROPE_THETA = 10000.0
NORM_EPS = 1e-6

LANES = 128
SUBLANES = 8
PACKED_SUBLANES = 16
VMEM_LIMIT = 56 * 1024 * 1024

F32 = jnp.float32
BF16 = jnp.bfloat16
NEG_INF = float("-inf")
NT_DIMS = (((1,), (1,)), ((), ()))
LOG2E = float(np.log2(np.e))
V_ROWS = HEAD_DIM + SUBLANES


def _sigmoid(x):
    return 0.5 * jnp.tanh(0.5 * x) + 0.5


def _cparams(n_grid):
    return pltpu.CompilerParams(
        dimension_semantics=("arbitrary",) * n_grid,
        vmem_limit_bytes=VMEM_LIMIT)


def _rope_table_kernel(pos_ref, inv_ref, cos_ref, sin_ref):
    ang = pos_ref[...].astype(F32) * inv_ref[...]
    cos, sin = jnp.cos(ang), jnp.sin(ang)
    cos_r, sin_r = pltpu.roll(cos, LANES // 2, 1), pltpu.roll(sin, LANES // 2, 1)
    low = lax.broadcasted_iota(jnp.int32, (1, LANES), 1) < LANES // 2
    cos_ref[:, :LANES] = jnp.where(low, cos, cos_r)
    cos_ref[:, LANES:] = jnp.where(low, cos_r, cos)
    sin_ref[:, :LANES] = jnp.where(low, -sin, sin_r)
    sin_ref[:, LANES:] = jnp.where(low, -sin_r, sin)


def _rope_tables(positions):
    m = positions.size
    inv_a = ROPE_THETA ** (-jnp.arange(0, HEAD_DIM, 2, dtype=F32) / HEAD_DIM)
    inv_i = ROPE_THETA ** (-jnp.arange(0, IDX_DIM, 2, dtype=F32) / IDX_DIM)
    inv = jnp.concatenate([inv_a, inv_i, inv_i])[None, :]
    tm = 2048
    return pl.pallas_call(
        _rope_table_kernel,
        grid=(m // tm,),
        in_specs=[pl.BlockSpec((tm, 1), lambda i: (i, 0)),
                  pl.BlockSpec((1, LANES), lambda i: (0, 0))],
        out_specs=[pl.BlockSpec((tm, 2 * LANES), lambda i: (i, 0))] * 2,
        out_shape=[jax.ShapeDtypeStruct((m, 2 * LANES), F32)] * 2,
        compiler_params=_cparams(1),
        name="rope_tables",
    )(positions.reshape(m, 1), inv)


def _rmsnorm_kernel(x_ref, g_ref, o_ref):
    x = x_ref[...]
    y = x * lax.rsqrt(jnp.mean(x * x, axis=-1, keepdims=True) + NORM_EPS)
    o_ref[...] = (y * g_ref[...]).astype(o_ref.dtype)


def _rmsnorm(x2d, g, out_dtype):
    m, d = x2d.shape
    tm = 1024
    return pl.pallas_call(
        _rmsnorm_kernel,
        grid=(m // tm,),
        in_specs=[pl.BlockSpec((tm, d), lambda i: (i, 0)),
                  pl.BlockSpec((1, d), lambda i: (0, 0))],
        out_specs=pl.BlockSpec((tm, d), lambda i: (i, 0)),
        out_shape=jax.ShapeDtypeStruct((m, d), out_dtype),
        compiler_params=_cparams(1),
        name="rmsnorm",
    )(x2d, g.reshape(1, d))


ROPE_TN = 512


ROPE_TILES = 7


def _rope_epilogue(acc, jp, cos_ref, sin_ref, o_ref):
    cos_a, sin_a = cos_ref[:, :LANES], sin_ref[:, :LANES]
    cos_i, sin_i = cos_ref[:, LANES:], sin_ref[:, LANES:]
    q_scale = (HEAD_DIM ** -0.5) * LOG2E
    use_a = jnp.where(jp < 2, q_scale, jnp.where(jp < 4, 1.0, 0.0)).astype(F32)
    idx_tile = jnp.logical_and(jp >= 4, jp < 6)
    tail = jp == ROPE_TILES - 1
    for c in range(ROPE_TN // LANES):
        ik_lanes = c < 2
        use_i = jnp.where(jnp.logical_or(idx_tile, jnp.logical_and(tail, ik_lanes)),
                          1.0, 0.0).astype(F32)
        ident = jnp.where(jnp.logical_and(tail, not ik_lanes), 1.0, 0.0).astype(F32)
        cos = cos_a * use_a + cos_i * use_i + ident
        sin = sin_a * use_a + sin_i * use_i
        x = acc[:, c * LANES:(c + 1) * LANES]
        o_ref[:, c * LANES:(c + 1) * LANES] = (
            x * cos + pltpu.roll(x, LANES // 2, 1) * sin).astype(o_ref.dtype)


def _proj_rope_kernel(h_ref, w_ref, cos_ref, sin_ref, o_ref, acc0_ref, acc1_ref):
    step = pl.program_id(0)
    jp = jnp.maximum(step - 1, 0) % ROPE_TILES

    @pl.when(step == 0)
    def _():
        acc1_ref[...] = jnp.zeros(acc1_ref.shape, F32)

    def body(acc_new, acc_old):
        acc_new[...] = jnp.dot(h_ref[...], w_ref[...], preferred_element_type=F32)
        _rope_epilogue(acc_old[...], jp, cos_ref, sin_ref, o_ref)

    @pl.when(step % 2 == 0)
    def _():
        body(acc0_ref, acc1_ref)

    @pl.when(step % 2 == 1)
    def _():
        body(acc1_ref, acc0_ref)


def _proj_rope(h, w, cos, sin):
    m, d = h.shape
    n = w.shape[1]
    assert n == ROPE_TILES * ROPE_TN
    tm = 1024
    n_steps = (m // tm) * ROPE_TILES
    cur = lambda s: jnp.minimum(s, n_steps - 1)
    prev = lambda s: jnp.maximum(s - 1, 0)
    return pl.pallas_call(
        _proj_rope_kernel,
        grid=(n_steps + 1,),
        in_specs=[pl.BlockSpec((tm, d), lambda s: (cur(s) // ROPE_TILES, 0)),
                  pl.BlockSpec((d, ROPE_TN), lambda s: (0, cur(s) % ROPE_TILES)),
                  pl.BlockSpec((tm, 2 * LANES), lambda s: (prev(s) // ROPE_TILES, 0)),
                  pl.BlockSpec((tm, 2 * LANES), lambda s: (prev(s) // ROPE_TILES, 0))],
        out_specs=pl.BlockSpec(
            (tm, ROPE_TN), lambda s: (prev(s) // ROPE_TILES, prev(s) % ROPE_TILES)),
        out_shape=jax.ShapeDtypeStruct((m, n), BF16),
        scratch_shapes=[pltpu.VMEM((tm, ROPE_TN), F32)] * 2,
        compiler_params=_cparams(1),
        name="proj_rope",
    )(h, w, cos, sin)


def _proj_plain_kernel(h_ref, w_ref, o_ref):
    o_ref[...] = jnp.dot(h_ref[...], w_ref[...],
                         preferred_element_type=F32).astype(o_ref.dtype)


def _proj_plain(h, w, tn, name):
    m, d = h.shape
    n = w.shape[1]
    tm = 1024
    return pl.pallas_call(
        _proj_plain_kernel,
        grid=(m // tm, n // tn),
        in_specs=[pl.BlockSpec((tm, d), lambda i, j: (i, 0)),
                  pl.BlockSpec((d, tn), lambda i, j: (0, j))],
        out_specs=pl.BlockSpec((tm, tn), lambda i, j: (i, j)),
        out_shape=jax.ShapeDtypeStruct((m, n), BF16),
        compiler_params=_cparams(2),
        name=name,
    )(h, w)


ATT_TK = 256


def _proj_vt_kernel(h_ref, wt_ref, o_ref):
    vt = lax.dot_general(wt_ref[...], h_ref[...], NT_DIMS,
                         preferred_element_type=F32)
    row = lax.broadcasted_iota(jnp.int32, (vt.shape[0], 1), 0)
    o_ref[...] = jnp.where(row % V_ROWS == HEAD_DIM, 1.0, vt).astype(o_ref.dtype)


def _proj_vt(h, wt, b, s):
    d = h.shape[1]
    n = wt.shape[0]
    tk = ATT_TK
    return pl.pallas_call(
        _proj_vt_kernel,
        grid=(b, s // tk),
        in_specs=[pl.BlockSpec((None, tk, d), lambda bi, t: (bi, t, 0)),
                  pl.BlockSpec((n, d), lambda bi, t: (0, 0))],
        out_specs=pl.BlockSpec((None, None, n, tk), lambda bi, t: (bi, t, 0, 0)),
        out_shape=jax.ShapeDtypeStruct((b, s // tk, n, tk), BF16),
        compiler_params=_cparams(2),
        name="proj_vt",
    )(h.reshape(b, s, d), wt)


COARSE_BITS = 16
FINE_UNCHECKED = 10
N_CHAINS = 4
M_INIT = -1e30


def _key16_to_f32(u):
    key = u ^ 0x8000
    bits = jnp.where((key & 0x8000) != 0, key ^ 0x7FFF, key)
    return pltpu.bitcast(jnp.left_shift(bits, 16), F32)


def _f32_to_key32(x):
    bits = pltpu.bitcast(x, jnp.int32)
    return jnp.where(bits < 0, bits ^ 0x7FFFFFFF, bits)


def _key32_to_f32(key):
    return pltpu.bitcast(jnp.where(key < 0, key ^ 0x7FFFFFFF, key), F32)


def _dsa_kernel(q_ref, iq_ref, k_ref, ikw_ref, vt_ref, ga_ref, o_ref,
                sc_ref, sc16_ref, bias_ref, acc_ref, p_ref, *, topk):
    i = pl.program_id(1)
    tk = ATT_TK
    nk = ((i + 1) * Q_BLOCK + tk - 1) // tk
    q0 = pl.multiple_of(i * Q_BLOCK, Q_BLOCK)
    qidx = q0 + lax.broadcasted_iota(jnp.int32, (1, Q_BLOCK), 1)

    def tile_start(t):
        return pl.multiple_of(t * tk, tk)

    def key_index(k0):
        return k0 + lax.broadcasted_iota(jnp.int32, (tk, 1), 0)

    iq = iq_ref[...]
    n_pair = IDX_HEADS // 2
    lhs = jnp.concatenate(
        [iq[:, p * LANES:(p + 1) * LANES] for p in range(n_pair)], axis=0)
    w = ikw_ref[pl.ds(q0, Q_BLOCK), 2 * LANES:3 * LANES].astype(F32)
    wt = w.T * ((IDX_HEADS ** -0.5) * (IDX_DIM ** -0.5))
    wrow = [wt[h:h + 1, :] for h in range(IDX_HEADS)]

    def score_tile(t, carry):
        k0 = tile_start(t)
        acc = jnp.zeros((tk, Q_BLOCK), F32)
        for e in range(2):
            ik_e = ikw_ref[pl.ds(k0, tk), e * LANES:(e + 1) * LANES]
            d = lax.dot_general(ik_e, lhs, NT_DIMS,
                                preferred_element_type=F32)
            for p in range(n_pair):
                dp = d[:, p * Q_BLOCK:(p + 1) * Q_BLOCK]
                acc = acc + wrow[2 * p + e] * jnp.maximum(dp, 0.0)
        sc = jnp.where(key_index(k0) <= qidx, acc, NEG_INF)
        sc_ref[pl.ds(k0, tk), :] = sc
        sc16_ref[pl.ds(k0, tk), :] = sc.astype(BF16)
        return carry

    lax.fori_loop(0, nk, score_tile, 0)

    one16 = jnp.ones((), BF16)
    zero16 = jnp.zeros((), BF16)
    n_packed = tk // PACKED_SUBLANES

    def coarse(it, t_u):
        cand_u = t_u | jnp.left_shift(jnp.int32(1), COARSE_BITS - 1 - it)
        cand = _key16_to_f32(cand_u).astype(BF16)

        def count_tile(t, cnts):
            x = sc16_ref[pl.ds(tile_start(t), tk), :]
            ind = jnp.where(x >= cand, one16, zero16)
            cnts = list(cnts)
            for j in range(n_packed):
                cnts[j % N_CHAINS] = cnts[j % N_CHAINS] + ind[
                    j * PACKED_SUBLANES:(j + 1) * PACKED_SUBLANES, :]
            return tuple(cnts)

        cnts = lax.fori_loop(
            0, nk, count_tile,
            (jnp.zeros((PACKED_SUBLANES, Q_BLOCK), BF16),) * N_CHAINS)
        total = jnp.sum(sum(c.astype(F32) for c in cnts), axis=0, keepdims=True)
        return jnp.where(total >= topk, cand_u, t_u)

    t_u = lax.fori_loop(0, COARSE_BITS, coarse,
                        jnp.zeros((1, Q_BLOCK), jnp.int32))

    def count_ge(cand):
        def count_tile(t, cnts):
            s = sc_ref[pl.ds(tile_start(t), tk), :]
            ind = jnp.where(s >= cand, 1.0, 0.0)
            cnts = list(cnts)
            for j in range(tk // SUBLANES):
                cnts[j % N_CHAINS] = cnts[j % N_CHAINS] + ind[
                    j * SUBLANES:(j + 1) * SUBLANES, :]
            return tuple(cnts)

        cnts = lax.fori_loop(
            0, nk, count_tile,
            (jnp.zeros((SUBLANES, Q_BLOCK), F32),) * N_CHAINS)
        return jnp.sum(sum(cnts), axis=0, keepdims=True)

    base = _f32_to_key32(_key16_to_f32(t_u))
    lo0 = base - 0x8000
    hi0 = base + 0x18001

    def fine_cond(st):
        lo, hi, cnt_lo = st
        open_ = jnp.logical_and(cnt_lo != topk, hi - lo > 1)
        return jnp.max(jnp.where(open_, 1.0, 0.0)) > 0.0

    def fine_step(st):
        lo, hi, cnt_lo = st
        mid = lo + jnp.right_shift(hi - lo, 1)
        cnt = count_ge(_key32_to_f32(mid))
        ge = cnt >= topk
        return (jnp.where(ge, mid, lo), jnp.where(ge, hi, mid),
                jnp.where(ge, cnt, cnt_lo))

    st = (lo0, hi0, jnp.full((1, Q_BLOCK), -1.0, F32))
    st = lax.fori_loop(0, FINE_UNCHECKED, lambda _, s_: fine_step(s_), st)
    lo, _, cnt_lo = lax.while_loop(
        fine_cond, lambda s_: fine_step(fine_step(s_)), st)
    full_row = qidx < topk
    thr = jnp.where(full_row, NEG_INF, _key32_to_f32(lo))

    def causal_bias(k0):
        return jnp.where(key_index(k0) <= qidx, 0.0, NEG_INF)

    def write_bias():
        def bias_tile(t, carry):
            k0 = tile_start(t)
            s = sc_ref[pl.ds(k0, tk), :]
            bias_ref[pl.ds(k0, tk), :] = jnp.where(s >= thr, causal_bias(k0), NEG_INF)
            return carry

        lax.fori_loop(0, nk, bias_tile, 0)

    def write_bias_with_ties():
        def count_where(pred):
            def count_tile(t, cnt):
                k0 = tile_start(t)
                ind = jnp.where(pred(sc_ref[pl.ds(k0, tk), :], key_index(k0)), 1.0, 0.0)
                return cnt + jnp.sum(ind, axis=0, keepdims=True)

            return lax.fori_loop(0, nk, count_tile, jnp.zeros((1, Q_BLOCK), F32))

        n_tied = topk - count_where(lambda s, kidx: s > thr)

        def index_step(_, st_):
            lo_i, hi_i = st_
            mid = lo_i + jnp.right_shift(hi_i - lo_i, 1)
            cnt = count_where(
                lambda s, kidx: jnp.logical_and(s == thr, kidx <= mid))
            ge = cnt >= n_tied
            return jnp.where(ge, lo_i, mid), jnp.where(ge, mid, hi_i)

        n_keys = nk * tk
        steps = int(np.ceil(np.log2(sc_ref.shape[0]))) + 1
        _, last = lax.fori_loop(
            0, steps, index_step,
            (jnp.full((1, Q_BLOCK), -1, jnp.int32),
             jnp.full((1, Q_BLOCK), 1, jnp.int32) * (n_keys - 1)))

        def bias_tile(t, carry):
            k0 = tile_start(t)
            s = sc_ref[pl.ds(k0, tk), :]
            keep = jnp.logical_or(
                s > thr, jnp.logical_and(s == thr, key_index(k0) <= last))
            bias_ref[pl.ds(k0, tk), :] = jnp.where(keep, causal_bias(k0), NEG_INF)
            return carry

        lax.fori_loop(0, nk, bias_tile, 0)

    tied = jnp.logical_and(cnt_lo != topk, jnp.logical_not(full_row))
    lax.cond(jnp.max(jnp.where(tied, 1.0, 0.0)) > 0.0,
             write_bias_with_ties, write_bias)

    acc_ref[...] = jnp.zeros(acc_ref.shape, F32)

    def qk_head(t, h, m_old):
        k0 = tile_start(t)
        hs = slice(h * HEAD_DIM, (h + 1) * HEAD_DIM)
        s = lax.dot_general(k_ref[pl.ds(k0, tk), hs], q_ref[:, hs], NT_DIMS,
                            preferred_element_type=F32)
        s = s + bias_ref[pl.ds(k0, tk), :]
        m_new = jnp.maximum(m_old, jnp.max(s, axis=0, keepdims=True))
        p_ref[t % 2, h] = jnp.exp2(s - m_new).astype(BF16)
        return m_new, jnp.exp2(m_old - m_new)

    def pv_head(t, h, alpha):
        pv = jnp.dot(vt_ref[t, h * V_ROWS:(h + 1) * V_ROWS, :], p_ref[t % 2, h],
                     preferred_element_type=F32)
        acc_ref[h] = acc_ref[h] * alpha + pv

    def qk_stage(t, ms):
        pairs = [qk_head(t, h, ms[h]) for h in range(N_HEADS)]
        return tuple(p[0] for p in pairs), tuple(p[1] for p in pairs)

    def pv_stage(t, alphas):
        for h in range(N_HEADS):
            pv_head(t, h, alphas[h])

    def att_tile(t, carry):
        ms, alphas = carry
        pairs = []
        for h in range(N_HEADS):
            pv_head(t - 1, h, alphas[h])
            pairs.append(qk_head(t, h, ms[h]))
        return tuple(p[0] for p in pairs), tuple(p[1] for p in pairs)

    m0 = tuple(jnp.full((1, Q_BLOCK), M_INIT, F32) for _ in range(N_HEADS))
    _, alphas = lax.fori_loop(1, nk, att_tile, qk_stage(0, m0))
    pv_stage(nk - 1, alphas)

    for h in range(N_HEADS):
        hs = slice(h * HEAD_DIM, (h + 1) * HEAD_DIM)
        out_t = acc_ref[h, :HEAD_DIM, :] / acc_ref[h, HEAD_DIM:HEAD_DIM + 1, :]
        gate = ga_ref[:, hs].astype(F32)
        o_ref[:, hs] = (out_t.T * (gate * _sigmoid(gate))).astype(o_ref.dtype)


def _dsa_attention(pr, pv, vt, b, s):
    pr3 = pr.reshape(b, s, pr.shape[1])
    pv3 = pv.reshape(b, s, pv.shape[1])
    topk = min(TOPK_MAX, s // 4)
    w_att = ATTN_WIDTH
    single = pl.Buffered(1)
    return pl.pallas_call(
        functools.partial(_dsa_kernel, topk=topk),
        grid=(b, s // Q_BLOCK),
        in_specs=[
            pl.BlockSpec((None, Q_BLOCK, w_att), lambda bi, i: (bi, i, 0)),
            pl.BlockSpec((None, Q_BLOCK, w_att), lambda bi, i: (bi, i, 2)),
            pl.BlockSpec((None, s, w_att), lambda bi, i: (bi, 0, 1),
                         pipeline_mode=single),
            pl.BlockSpec((None, s, 3 * LANES), lambda bi, i: (bi, 0, 8),
                         pipeline_mode=single),
            pl.BlockSpec((None, s // ATT_TK, N_HEADS * V_ROWS, ATT_TK),
                         lambda bi, i: (bi, 0, 0, 0), pipeline_mode=single),
            pl.BlockSpec((None, Q_BLOCK, w_att), lambda bi, i: (bi, i, 0)),
        ],
        out_specs=pl.BlockSpec((None, Q_BLOCK, w_att), lambda bi, i: (bi, i, 0)),
        out_shape=jax.ShapeDtypeStruct((b, s, w_att), BF16),
        scratch_shapes=[pltpu.VMEM((s, Q_BLOCK), F32),
                        pltpu.VMEM((s, Q_BLOCK), BF16),
                        pltpu.VMEM((s, Q_BLOCK), F32),
                        pltpu.VMEM((N_HEADS, V_ROWS, Q_BLOCK), F32),
                        pltpu.VMEM((2, N_HEADS, ATT_TK, Q_BLOCK), BF16)],
        compiler_params=_cparams(2),
        name="dsa_attention",
    )(pr3, pr3, pr3, pr3, vt, pv3)


RNN_T = 512
N_RNN_TILES = D_RNN // LANES
GATE_WIN = 3


def _gate_window_start(j):
    assert RNN_BLOCK_DIM <= LANES
    return min(max(j - 1, 0), N_RNN_TILES - GATE_WIN)


def _rglru_kernel(xr_ref, gr_ref, cw_ref, cb_ref, wg_ref, bg_ref, lam_ref,
                  o_ref, xp_ref, a_ref, u_ref, h_ref, hc_ref):
    c = pl.program_id(1)
    t_len = RNN_T
    pad = SUBLANES

    @pl.when(c == 0)
    def _():
        xp_ref[0:pad, :] = jnp.zeros((pad, D_RNN), F32)
        hc_ref[...] = jnp.zeros((1, D_RNN), F32)

    x = xr_ref[...].astype(F32)
    xp_ref[pad:pad + t_len, :] = x
    y = cb_ref[...] + cw_ref[CONV_WIDTH - 1:CONV_WIDTH, :] * x
    for j in range(CONV_WIDTH - 1):
        back = CONV_WIDTH - 1 - j
        y = y + cw_ref[j:j + 1, :] * xp_ref[pad - back:pad - back + t_len, :]
    xp_ref[0:pad, :] = x[t_len - pad:, :]

    y16 = y.astype(BF16)
    yh = 0.5 * y
    z = -lam_ref[...]
    softplus = jnp.maximum(z, 0.0) + jnp.log1p(jnp.exp(-jnp.abs(z)))
    k = (-0.5 * LRU_C * LOG2E) * softplus
    for j in range(N_RNN_TILES):
        ls = slice(j * LANES, (j + 1) * LANES)
        w0 = _gate_window_start(j) * LANES
        g = jnp.dot(y16[:, w0:w0 + GATE_WIN * LANES], wg_ref[j],
                    preferred_element_type=F32) + bg_ref[j]
        t_r = jnp.tanh(g[:, :LANES])
        t_i = jnp.tanh(g[:, LANES:])
        a = jnp.exp2(k[:, ls] * t_r + k[:, ls])
        a_ref[:, ls] = a
        u_ref[:, ls] = jnp.sqrt(1.0 - a * a) * (t_i * yh[:, ls] + yh[:, ls])

    def step(t, h):
        h = a_ref[pl.ds(t, 1), :] * h + u_ref[pl.ds(t, 1), :]
        h_ref[pl.ds(t, 1), :] = h
        return h

    hc_ref[...] = lax.fori_loop(0, t_len, step, hc_ref[...], unroll=8)
    gh = 0.5 * gr_ref[...].astype(F32)
    o_ref[...] = (h_ref[...] * (gh * jnp.tanh(gh) + gh)).astype(o_ref.dtype)


def _rglru(pc, b, s, conv_w, conv_b, w_gates, b_gates, lam):
    pc3 = pc.reshape(b, s, pc.shape[1])
    t = RNN_T
    const = lambda bi, c: (0, 0)
    const3 = lambda bi, c: (0, 0, 0)
    return pl.pallas_call(
        _rglru_kernel,
        grid=(b, s // t),
        in_specs=[
            pl.BlockSpec((None, t, D_RNN), lambda bi, c: (bi, c, 0)),
            pl.BlockSpec((None, t, D_RNN), lambda bi, c: (bi, c, 1)),
            pl.BlockSpec((CONV_WIDTH, D_RNN), const),
            pl.BlockSpec((1, D_RNN), const),
            pl.BlockSpec((N_RNN_TILES, GATE_WIN * LANES, 2 * LANES), const3),
            pl.BlockSpec((N_RNN_TILES, 1, 2 * LANES), const3),
            pl.BlockSpec((1, D_RNN), const),
        ],
        out_specs=pl.BlockSpec((None, t, D_RNN), lambda bi, c: (bi, c, 0)),
        out_shape=jax.ShapeDtypeStruct((b, s, D_RNN), BF16),
        scratch_shapes=[pltpu.VMEM((t + SUBLANES, D_RNN), F32),
                        pltpu.VMEM((t, D_RNN), F32),
                        pltpu.VMEM((t, D_RNN), F32),
                        pltpu.VMEM((t, D_RNN), F32),
                        pltpu.VMEM((1, D_RNN), F32)],
        compiler_params=_cparams(2),
        name="rglru",
    )(pc3, pc3, conv_w, conv_b.reshape(1, D_RNN), w_gates, b_gates,
      lam.reshape(1, D_RNN))


def _pack_gates(w_rg_l, b_rg_l, w_ig_l, b_ig_l):
    dense = [jax.scipy.linalg.block_diag(*w) for w in (w_rg_l, w_ig_l)]
    bands, biases = [], []
    for j in range(N_RNN_TILES):
        w0 = _gate_window_start(j) * LANES
        cols = slice(j * LANES, (j + 1) * LANES)
        bands.append(jnp.concatenate(
            [dm[w0:w0 + GATE_WIN * LANES, cols] for dm in dense], axis=1))
        biases.append(jnp.concatenate([b_rg_l[cols], b_ig_l[cols]])[None, :])
    return (0.5 * jnp.stack(bands)).astype(BF16), 0.5 * jnp.stack(biases)


def _merge_kernel(at_ref, hr_ref, ma_ref, mb_ref, x_ref, woa_ref, wor_ref,
                  wo_ref, g_ref, xo_ref, no_ref):
    y_a = jnp.dot(at_ref[...], woa_ref[...], preferred_element_type=F32)
    y_b = jnp.dot(hr_ref[...], wor_ref[...], preferred_element_type=F32)
    merged = (_sigmoid(ma_ref[...].astype(F32)) * y_a
              + _sigmoid(mb_ref[...].astype(F32)) * y_b)
    x = x_ref[...] + jnp.dot(merged.astype(BF16), wo_ref[...],
                             preferred_element_type=F32)
    xo_ref[...] = x
    y = x * lax.rsqrt(jnp.mean(x * x, axis=-1, keepdims=True) + NORM_EPS)
    no_ref[...] = (y * g_ref[...]).astype(no_ref.dtype)


def _merge(attn_g, hr_g, pv, x2d, w_oa, w_or, w_o, g, norm_dtype):
    m, d = x2d.shape
    tm = 512
    const = lambda i: (0, 0)
    return pl.pallas_call(
        _merge_kernel,
        grid=(m // tm,),
        in_specs=[
            pl.BlockSpec((tm, ATTN_WIDTH), lambda i: (i, 0)),
            pl.BlockSpec((tm, D_RNN), lambda i: (i, 0)),
            pl.BlockSpec((tm, d), lambda i: (i, 1)),
            pl.BlockSpec((tm, d), lambda i: (i, 2)),
            pl.BlockSpec((tm, d), lambda i: (i, 0)),
            pl.BlockSpec((ATTN_WIDTH, d), const),
            pl.BlockSpec((D_RNN, d), const),
            pl.BlockSpec((d, d), const),
            pl.BlockSpec((1, d), const),
        ],
        out_specs=[pl.BlockSpec((tm, d), lambda i: (i, 0))] * 2,
        out_shape=[jax.ShapeDtypeStruct((m, d), F32),
                   jax.ShapeDtypeStruct((m, d), norm_dtype)],
        compiler_params=_cparams(1),
        name="merge",
    )(attn_g, hr_g, pv, pv, x2d, w_oa, w_or, w_o, g.reshape(1, d))


def _pack_weights(w_in_l):
    offs = np.cumsum((ATTN_WIDTH,) * 4 + (IDX_HEADS * IDX_DIM, IDX_DIM, IDX_HEADS,
                                          D_RNN, D_RNN, D_MODEL, D_MODEL))[:-1]
    w16 = w_in_l.astype(BF16)
    q, k, v, ga, iq, ik, iw, xr, gr, ma, mb = jnp.split(w16, offs.tolist(), axis=1)
    d = w_in_l.shape[0]
    half = IDX_DIM // 2
    z = lambda n: jnp.zeros((d, n), BF16)
    iq = iq.reshape(d, IDX_HEADS // 2, 2, 2, half).transpose(0, 1, 3, 2, 4).reshape(
        d, IDX_HEADS * IDX_DIM)
    ik1, ik2 = ik[:, :half], ik[:, half:]
    tail = jnp.concatenate([ik1, z(half), ik2, z(half), z(half), ik1, z(half), ik2,
                            iw, z(ROPE_TN - 2 * LANES - IDX_HEADS)], axis=1)
    w_rope = jnp.concatenate([q, k, iq, tail], axis=1)
    w_gate = jnp.concatenate([ga, ma, mb], axis=1)
    vt = v.T.reshape(N_HEADS, HEAD_DIM, d)
    w_vt = jnp.pad(vt, ((0, 0), (0, V_ROWS - HEAD_DIM), (0, 0))).reshape(
        N_HEADS * V_ROWS, d)
    w_rnn = jnp.concatenate([xr, gr], axis=1)
    return w_rope, w_gate, w_vt, w_rnn


def kernel(x, positions, norm_g, w_in, conv_w, conv_b, w_rg, b_rg, w_ig, b_ig,
           lru_lambda, w_out_attn, w_out_rnn, w_o, final_g):
    b, s, d = x.shape
    depth = w_in.shape[0]
    m = b * s
    cos, sin = _rope_tables(positions)
    x2d = x.reshape(m, d)
    h = _rmsnorm(x2d, norm_g[0], BF16)
    out = None
    for l in range(depth):
        w_rope, w_gate, w_vt, w_rnn = _pack_weights(w_in[l])
        w_gates, b_gates = _pack_gates(w_rg[l], b_rg[l], w_ig[l], b_ig[l])
        pr = _proj_rope(h, w_rope, cos, sin)
        pv = _proj_plain(h, w_gate, 1024, "proj_gate")
        vt = _proj_vt(h, w_vt, b, s)
        pc = _proj_plain(h, w_rnn, D_RNN, "proj_rnn")
        attn_g = _dsa_attention(pr, pv, vt, b, s).reshape(m, ATTN_WIDTH)
        hr_g = _rglru(pc, b, s, conv_w[l], conv_b[l], w_gates, b_gates,
                      lru_lambda[l]).reshape(m, D_RNN)
        last = l == depth - 1
        g_next = final_g if last else norm_g[l + 1]
        x2d, normed = _merge(attn_g, hr_g, pv, x2d,
                             w_out_attn[l].astype(BF16), w_out_rnn[l].astype(BF16),
                             w_o[l].astype(BF16), g_next, F32 if last else BF16)
        if last:
            out = normed
        else:
            h = normed
    return out.reshape(b, s, d)
```

```python
import functools

import jax
import jax.numpy as jnp
import numpy as np
from jax import lax
from jax.experimental import pallas as pl
from jax.experimental.pallas import tpu as pltpu

D_MODEL = 1024
N_HEADS = 8
HEAD_DIM = 128
ATTN_WIDTH = N_HEADS * HEAD_DIM
IDX_HEADS = 16
IDX_DIM = 64
TOPK_MAX = 256
Q_BLOCK = 256
D_RNN = 1408
RNN_BLOCKS = 16
RNN_BLOCK_DIM = D_RNN // RNN_BLOCKS
CONV_WIDTH = 4
LRU_C = 8.0
ROPE_THETA = 10000.0
NORM_EPS = 1e-6

LANES = 128
SUBLANES = 8
PACKED_SUBLANES = 16
VMEM_LIMIT = 56 * 1024 * 1024

F32 = jnp.float32
BF16 = jnp.bfloat16
NEG_INF = float("-inf")
NT_DIMS = (((1,), (1,)), ((), ()))
LOG2E = float(np.log2(np.e))
V_ROWS = HEAD_DIM + SUBLANES


def _sigmoid(x):
    return 0.5 * jnp.tanh(0.5 * x) + 0.5


def _cparams(n_grid):
    return pltpu.CompilerParams(
        dimension_semantics=("arbitrary",) * n_grid,
        vmem_limit_bytes=VMEM_LIMIT)


def _rope_table_kernel(pos_ref, inv_ref, cos_ref, sin_ref):
    ang = pos_ref[...].astype(F32) * inv_ref[...]
    cos, sin = jnp.cos(ang), jnp.sin(ang)
    cos_r, sin_r = pltpu.roll(cos, LANES // 2, 1), pltpu.roll(sin, LANES // 2, 1)
    low = lax.broadcasted_iota(jnp.int32, (1, LANES), 1) < LANES // 2
    cos_ref[:, :LANES] = jnp.where(low, cos, cos_r)
    cos_ref[:, LANES:] = jnp.where(low, cos_r, cos)
    sin_ref[:, :LANES] = jnp.where(low, -sin, sin_r)
    sin_ref[:, LANES:] = jnp.where(low, -sin_r, sin)


def _rope_tables(positions):
    m = positions.size
    inv_a = ROPE_THETA ** (-jnp.arange(0, HEAD_DIM, 2, dtype=F32) / HEAD_DIM)
    inv_i = ROPE_THETA ** (-jnp.arange(0, IDX_DIM, 2, dtype=F32) / IDX_DIM)
    inv = jnp.concatenate([inv_a, inv_i, inv_i])[None, :]
    tm = 2048
    return pl.pallas_call(
        _rope_table_kernel,
        grid=(m // tm,),
        in_specs=[pl.BlockSpec((tm, 1), lambda i: (i, 0)),
                  pl.BlockSpec((1, LANES), lambda i: (0, 0))],
        out_specs=[pl.BlockSpec((tm, 2 * LANES), lambda i: (i, 0))] * 2,
        out_shape=[jax.ShapeDtypeStruct((m, 2 * LANES), F32)] * 2,
        compiler_params=_cparams(1),
        name="rope_tables",
    )(positions.reshape(m, 1), inv)


def _rmsnorm_kernel(x_ref, g_ref, o_ref):
    x = x_ref[...]
    y = x * lax.rsqrt(jnp.mean(x * x, axis=-1, keepdims=True) + NORM_EPS)
    o_ref[...] = (y * g_ref[...]).astype(o_ref.dtype)


def _rmsnorm(x2d, g, out_dtype):
    m, d = x2d.shape
    tm = 1024
    return pl.pallas_call(
        _rmsnorm_kernel,
        grid=(m // tm,),
        in_specs=[pl.BlockSpec((tm, d), lambda i: (i, 0)),
                  pl.BlockSpec((1, d), lambda i: (0, 0))],
        out_specs=pl.BlockSpec((tm, d), lambda i: (i, 0)),
        out_shape=jax.ShapeDtypeStruct((m, d), out_dtype),
        compiler_params=_cparams(1),
        name="rmsnorm",
    )(x2d, g.reshape(1, d))


ROPE_TN = 512


ROPE_TILES = 7


def _rope_epilogue(acc, jp, cos_ref, sin_ref, o_ref):
    cos_a, sin_a = cos_ref[:, :LANES], sin_ref[:, :LANES]
    cos_i, sin_i = cos_ref[:, LANES:], sin_ref[:, LANES:]
    q_scale = (HEAD_DIM ** -0.5) * LOG2E
    use_a = jnp.where(jp < 2, q_scale, jnp.where(jp < 4, 1.0, 0.0)).astype(F32)
    idx_tile = jnp.logical_and(jp >= 4, jp < 6)
    tail = jp == ROPE_TILES - 1
    for c in range(ROPE_TN // LANES):
        ik_lanes = c < 2
        use_i = jnp.where(jnp.logical_or(idx_tile, jnp.logical_and(tail, ik_lanes)),
                          1.0, 0.0).astype(F32)
        ident = jnp.where(jnp.logical_and(tail, not ik_lanes), 1.0, 0.0).astype(F32)
        cos = cos_a * use_a + cos_i * use_i + ident
        sin = sin_a * use_a + sin_i * use_i
        x = acc[:, c * LANES:(c + 1) * LANES]
        o_ref[:, c * LANES:(c + 1) * LANES] = (
            x * cos + pltpu.roll(x, LANES // 2, 1) * sin).astype(o_ref.dtype)


def _proj_rope_kernel(h_ref, w_ref, cos_ref, sin_ref, o_ref, acc0_ref, acc1_ref):
    step = pl.program_id(0)
    jp = jnp.maximum(step - 1, 0) % ROPE_TILES

    @pl.when(step == 0)
    def _():
        acc1_ref[...] = jnp.zeros(acc1_ref.shape, F32)

    def body(acc_new, acc_old):
        acc_new[...] = jnp.dot(h_ref[...], w_ref[...], preferred_element_type=F32)
        _rope_epilogue(acc_old[...], jp, cos_ref, sin_ref, o_ref)

    @pl.when(step % 2 == 0)
    def _():
        body(acc0_ref, acc1_ref)

    @pl.when(step % 2 == 1)
    def _():
        body(acc1_ref, acc0_ref)


def _proj_rope(h, w, cos, sin):
    m, d = h.shape
    n = w.shape[1]
    assert n == ROPE_TILES * ROPE_TN
    tm = 1024
    n_steps = (m // tm) * ROPE_TILES
    cur = lambda s: jnp.minimum(s, n_steps - 1)
    prev = lambda s: jnp.maximum(s - 1, 0)
    return pl.pallas_call(
        _proj_rope_kernel,
        grid=(n_steps + 1,),
        in_specs=[pl.BlockSpec((tm, d), lambda s: (cur(s) // ROPE_TILES, 0)),
                  pl.BlockSpec((d, ROPE_TN), lambda s: (0, cur(s) % ROPE_TILES)),
                  pl.BlockSpec((tm, 2 * LANES), lambda s: (prev(s) // ROPE_TILES, 0)),
                  pl.BlockSpec((tm, 2 * LANES), lambda s: (prev(s) // ROPE_TILES, 0))],
        out_specs=pl.BlockSpec(
            (tm, ROPE_TN), lambda s: (prev(s) // ROPE_TILES, prev(s) % ROPE_TILES)),
        out_shape=jax.ShapeDtypeStruct((m, n), BF16),
        scratch_shapes=[pltpu.VMEM((tm, ROPE_TN), F32)] * 2,
        compiler_params=_cparams(1),
        name="proj_rope",
    )(h, w, cos, sin)


def _proj_plain_kernel(h_ref, w_ref, o_ref):
    o_ref[...] = jnp.dot(h_ref[...], w_ref[...],
                         preferred_element_type=F32).astype(o_ref.dtype)


def _proj_plain(h, w, tn, name):
    m, d = h.shape
    n = w.shape[1]
    tm = 1024
    return pl.pallas_call(
        _proj_plain_kernel,
        grid=(m // tm, n // tn),
        in_specs=[pl.BlockSpec((tm, d), lambda i, j: (i, 0)),
                  pl.BlockSpec((d, tn), lambda i, j: (0, j))],
        out_specs=pl.BlockSpec((tm, tn), lambda i, j: (i, j)),
        out_shape=jax.ShapeDtypeStruct((m, n), BF16),
        compiler_params=_cparams(2),
        name=name,
    )(h, w)


ATT_TK = 256


VT_TILES = 2


def _proj_vt_kernel(h_ref, wt_ref, o_ref):
    vt = lax.dot_general(wt_ref[...], h_ref[...], NT_DIMS,
                         preferred_element_type=F32)
    row = lax.broadcasted_iota(jnp.int32, (vt.shape[0], 1), 0)
    vt = jnp.where(row % V_ROWS == HEAD_DIM, 1.0, vt).astype(o_ref.dtype)
    for j in range(VT_TILES):
        o_ref[j] = vt[:, j * ATT_TK:(j + 1) * ATT_TK]


def _proj_vt(h, wt, b, s):
    d = h.shape[1]
    n = wt.shape[0]
    tk = ATT_TK
    rows = VT_TILES * tk
    return pl.pallas_call(
        _proj_vt_kernel,
        grid=(b, s // rows),
        in_specs=[pl.BlockSpec((None, rows, d), lambda bi, t: (bi, t, 0)),
                  pl.BlockSpec((n, d), lambda bi, t: (0, 0))],
        out_specs=pl.BlockSpec((None, VT_TILES, n, tk), lambda bi, t: (bi, t, 0, 0)),
        out_shape=jax.ShapeDtypeStruct((b, s // tk, n, tk), BF16),
        compiler_params=_cparams(2),
        name="proj_vt",
    )(h.reshape(b, s, d), wt)


SCORE_TK = 512
COARSE_BITS = 16
FINE_UNCHECKED = 10
N_CHAINS = 4
M_INIT = -1e30


def _key16_to_f32(u):
    key = u ^ 0x8000
    bits = jnp.where((key & 0x8000) != 0, key ^ 0x7FFF, key)
    return pltpu.bitcast(jnp.left_shift(bits, 16), F32)


def _f32_to_key32(x):
    bits = pltpu.bitcast(x, jnp.int32)
    return jnp.where(bits < 0, bits ^ 0x7FFFFFFF, bits)


def _key32_to_f32(key):
    return pltpu.bitcast(jnp.where(key < 0, key ^ 0x7FFFFFFF, key), F32)


def _dsa_kernel(q_ref, iq_ref, k_ref, ikw_ref, vt_ref, ga_ref, o_ref,
                sc_ref, sc16_ref, bias_ref, acc_ref, p_ref, *, topk):
    i = pl.program_id(1)
    tk = ATT_TK
    nk = ((i + 1) * Q_BLOCK + tk - 1) // tk
    q0 = pl.multiple_of(i * Q_BLOCK, Q_BLOCK)
    qidx = q0 + lax.broadcasted_iota(jnp.int32, (1, Q_BLOCK), 1)

    def tile_start(t):
        return pl.multiple_of(t * tk, tk)

    def key_index(k0):
        return k0 + lax.broadcasted_iota(jnp.int32, (tk, 1), 0)

    iq = iq_ref[...]
    n_pair = IDX_HEADS // 2
    lhs = jnp.concatenate(
        [iq[:, p * LANES:(p + 1) * LANES] for p in range(n_pair)], axis=0)
    w = ikw_ref[pl.ds(q0, Q_BLOCK), 2 * LANES:3 * LANES].astype(F32)
    wt = w.T * ((IDX_HEADS ** -0.5) * (IDX_DIM ** -0.5))
    wrow = [wt[h:h + 1, :] for h in range(IDX_HEADS)]

    def score_tile(t, carry):
        k0 = pl.multiple_of(t * SCORE_TK, SCORE_TK)
        acc = jnp.zeros((SCORE_TK, Q_BLOCK), F32)
        for e in range(2):
            ik_e = ikw_ref[pl.ds(k0, SCORE_TK), e * LANES:(e + 1) * LANES]
            d = lax.dot_general(ik_e, lhs, NT_DIMS,
                                preferred_element_type=F32)
            for p in range(n_pair):
                dp = d[:, p * Q_BLOCK:(p + 1) * Q_BLOCK]
                acc = acc + wrow[2 * p + e] * jnp.maximum(dp, 0.0)
        kidx = k0 + lax.broadcasted_iota(jnp.int32, (SCORE_TK, 1), 0)
        sc = jnp.where(kidx <= qidx, acc, NEG_INF)
        sc_ref[pl.ds(k0, SCORE_TK), :] = sc
        sc16_ref[pl.ds(k0, SCORE_TK), :] = sc.astype(BF16)
        return carry

    lax.fori_loop(0, ((i + 1) * Q_BLOCK + SCORE_TK - 1) // SCORE_TK, score_tile, 0)

    one16 = jnp.ones((), BF16)
    zero16 = jnp.zeros((), BF16)
    n_packed = tk // PACKED_SUBLANES

    def coarse(it, t_u):
        cand_u = t_u | jnp.left_shift(jnp.int32(1), COARSE_BITS - 1 - it)
        cand = _key16_to_f32(cand_u).astype(BF16)

        def count_tile(t, cnts):
            x = sc16_ref[pl.ds(tile_start(t), tk), :]
            ind = jnp.where(x >= cand, one16, zero16)
            cnts = list(cnts)
            for j in range(n_packed):
                cnts[j % N_CHAINS] = cnts[j % N_CHAINS] + ind[
                    j * PACKED_SUBLANES:(j + 1) * PACKED_SUBLANES, :]
            return tuple(cnts)

        cnts = lax.fori_loop(
            0, nk, count_tile,
            (jnp.zeros((PACKED_SUBLANES, Q_BLOCK), BF16),) * N_CHAINS)
        total = jnp.sum(sum(c.astype(F32) for c in cnts), axis=0, keepdims=True)
        return jnp.where(total >= topk, cand_u, t_u)

    t_u = lax.fori_loop(0, COARSE_BITS, coarse,
                        jnp.zeros((1, Q_BLOCK), jnp.int32))

    def count_ge(cand):
        def count_tile(t, cnts):
            s = sc_ref[pl.ds(tile_start(t), tk), :]
            ind = jnp.where(s >= cand, 1.0, 0.0)
            cnts = list(cnts)
            for j in range(tk // SUBLANES):
                cnts[j % N_CHAINS] = cnts[j % N_CHAINS] + ind[
                    j * SUBLANES:(j + 1) * SUBLANES, :]
            return tuple(cnts)

        cnts = lax.fori_loop(
            0, nk, count_tile,
            (jnp.zeros((SUBLANES, Q_BLOCK), F32),) * N_CHAINS)
        return jnp.sum(sum(cnts), axis=0, keepdims=True)

    base = _f32_to_key32(_key16_to_f32(t_u))
    lo0 = base - 0x8000
    hi0 = base + 0x18001

    def fine_cond(st):
        lo, hi, cnt_lo = st
        open_ = jnp.logical_and(cnt_lo != topk, hi - lo > 1)
        return jnp.max(jnp.where(open_, 1.0, 0.0)) > 0.0

    def fine_step(st):
        lo, hi, cnt_lo = st
        mid = lo + jnp.right_shift(hi - lo, 1)
        cnt = count_ge(_key32_to_f32(mid))
        ge = cnt >= topk
        return (jnp.where(ge, mid, lo), jnp.where(ge, hi, mid),
                jnp.where(ge, cnt, cnt_lo))

    st = (lo0, hi0, jnp.full((1, Q_BLOCK), -1.0, F32))
    st = lax.fori_loop(0, FINE_UNCHECKED, lambda _, s_: fine_step(s_), st)
    lo, _, cnt_lo = lax.while_loop(
        fine_cond, lambda s_: fine_step(fine_step(s_)), st)
    full_row = qidx < topk
    thr = jnp.where(full_row, NEG_INF, _key32_to_f32(lo))

    def causal_bias(k0):
        return jnp.where(key_index(k0) <= qidx, 0.0, NEG_INF)

    def write_bias():
        def bias_tile(t, carry):
            k0 = tile_start(t)
            s = sc_ref[pl.ds(k0, tk), :]
            bias_ref[pl.ds(k0, tk), :] = jnp.where(s >= thr, causal_bias(k0), NEG_INF)
            return carry

        lax.fori_loop(0, nk, bias_tile, 0)

    def write_bias_with_ties():
        def count_where(pred):
            def count_tile(t, cnt):
                k0 = tile_start(t)
                ind = jnp.where(pred(sc_ref[pl.ds(k0, tk), :], key_index(k0)), 1.0, 0.0)
                return cnt + jnp.sum(ind, axis=0, keepdims=True)

            return lax.fori_loop(0, nk, count_tile, jnp.zeros((1, Q_BLOCK), F32))

        n_tied = topk - count_where(lambda s, kidx: s > thr)

        def index_step(_, st_):
            lo_i, hi_i = st_
            mid = lo_i + jnp.right_shift(hi_i - lo_i, 1)
            cnt = count_where(
                lambda s, kidx: jnp.logical_and(s == thr, kidx <= mid))
            ge = cnt >= n_tied
            return jnp.where(ge, lo_i, mid), jnp.where(ge, mid, hi_i)

        n_keys = nk * tk
        steps = int(np.ceil(np.log2(sc_ref.shape[0]))) + 1
        _, last = lax.fori_loop(
            0, steps, index_step,
            (jnp.full((1, Q_BLOCK), -1, jnp.int32),
             jnp.full((1, Q_BLOCK), 1, jnp.int32) * (n_keys - 1)))

        def bias_tile(t, carry):
            k0 = tile_start(t)
            s = sc_ref[pl.ds(k0, tk), :]
            keep = jnp.logical_or(
                s > thr, jnp.logical_and(s == thr, key_index(k0) <= last))
            bias_ref[pl.ds(k0, tk), :] = jnp.where(keep, causal_bias(k0), NEG_INF)
            return carry

        lax.fori_loop(0, nk, bias_tile, 0)

    tied = jnp.logical_and(cnt_lo != topk, jnp.logical_not(full_row))
    lax.cond(jnp.max(jnp.where(tied, 1.0, 0.0)) > 0.0,
             write_bias_with_ties, write_bias)

    acc_ref[...] = jnp.zeros(acc_ref.shape, F32)

    def qk_head(t, h, m_old):
        k0 = tile_start(t)
        hs = slice(h * HEAD_DIM, (h + 1) * HEAD_DIM)
        s = lax.dot_general(k_ref[pl.ds(k0, tk), hs], q_ref[:, hs], NT_DIMS,
                            preferred_element_type=F32)
        s = s + bias_ref[pl.ds(k0, tk), :]
        m_new = jnp.maximum(m_old, jnp.max(s, axis=0, keepdims=True))
        p_ref[t % 2, h] = jnp.exp2(s - m_new).astype(BF16)
        return m_new, jnp.exp2(m_old - m_new)

    def pv_head(t, h, alpha):
        pv = jnp.dot(vt_ref[t, h * V_ROWS:(h + 1) * V_ROWS, :], p_ref[t % 2, h],
                     preferred_element_type=F32)
        acc_ref[h] = acc_ref[h] * alpha + pv

    def qk_stage(t, ms):
        pairs = [qk_head(t, h, ms[h]) for h in range(N_HEADS)]
        return tuple(p[0] for p in pairs), tuple(p[1] for p in pairs)

    def pv_stage(t, alphas):
        for h in range(N_HEADS):
            pv_head(t, h, alphas[h])

    def att_tile(t, carry):
        ms, alphas = carry
        pairs = []
        for h in range(N_HEADS):
            pv_head(t - 1, h, alphas[h])
            pairs.append(qk_head(t, h, ms[h]))
        return tuple(p[0] for p in pairs), tuple(p[1] for p in pairs)

    m0 = tuple(jnp.full((1, Q_BLOCK), M_INIT, F32) for _ in range(N_HEADS))
    _, alphas = lax.fori_loop(1, nk, att_tile, qk_stage(0, m0))
    pv_stage(nk - 1, alphas)

    for h in range(N_HEADS):
        hs = slice(h * HEAD_DIM, (h + 1) * HEAD_DIM)
        out_t = acc_ref[h, :HEAD_DIM, :] * (1.0 / acc_ref[h, HEAD_DIM:HEAD_DIM + 1, :])
        gate = ga_ref[:, hs].astype(F32)
        o_ref[:, hs] = (out_t.T * (gate * _sigmoid(gate))).astype(o_ref.dtype)


def _dsa_attention(pr, pv, vt, b, s):
    pr3 = pr.reshape(b, s, pr.shape[1])
    pv3 = pv.reshape(b, s, pv.shape[1])
    topk = min(TOPK_MAX, s // 4)
    w_att = ATTN_WIDTH
    single = pl.Buffered(1)
    return pl.pallas_call(
        functools.partial(_dsa_kernel, topk=topk),
        grid=(b, s // Q_BLOCK),
        in_specs=[
            pl.BlockSpec((None, Q_BLOCK, w_att), lambda bi, i: (bi, i, 0)),
            pl.BlockSpec((None, Q_BLOCK, w_att), lambda bi, i: (bi, i, 2)),
            pl.BlockSpec((None, s, w_att), lambda bi, i: (bi, 0, 1),
                         pipeline_mode=single),
            pl.BlockSpec((None, s, 3 * LANES), lambda bi, i: (bi, 0, 8),
                         pipeline_mode=single),
            pl.BlockSpec((None, s // ATT_TK, N_HEADS * V_ROWS, ATT_TK),
                         lambda bi, i: (bi, 0, 0, 0), pipeline_mode=single),
            pl.BlockSpec((None, Q_BLOCK, w_att), lambda bi, i: (bi, i, 0)),
        ],
        out_specs=pl.BlockSpec((None, Q_BLOCK, w_att), lambda bi, i: (bi, i, 0)),
        out_shape=jax.ShapeDtypeStruct((b, s, w_att), BF16),
        scratch_shapes=[pltpu.VMEM((s, Q_BLOCK), F32),
                        pltpu.VMEM((s, Q_BLOCK), BF16),
                        pltpu.VMEM((s, Q_BLOCK), F32),
                        pltpu.VMEM((N_HEADS, V_ROWS, Q_BLOCK), F32),
                        pltpu.VMEM((2, N_HEADS, ATT_TK, Q_BLOCK), BF16)],
        compiler_params=_cparams(2),
        name="dsa_attention",
    )(pr3, pr3, pr3, pr3, vt, pv3)


RNN_T = 512
N_RNN_TILES = D_RNN // LANES
GATE_WIN = 3


def _gate_window_start(j):
    assert RNN_BLOCK_DIM <= LANES
    return min(max(j - 1, 0), N_RNN_TILES - GATE_WIN)


def _rglru_kernel(xr_ref, gr_ref, cw_ref, cb_ref, wg_ref, bg_ref, lam_ref,
                  o_ref, xp_ref, a_ref, u_ref, h_ref, hc_ref):
    c = pl.program_id(1)
    t_len = RNN_T
    pad = SUBLANES

    @pl.when(c == 0)
    def _():
        xp_ref[0:pad, :] = jnp.zeros((pad, D_RNN), F32)
        hc_ref[...] = jnp.zeros((1, D_RNN), F32)

    x = xr_ref[...].astype(F32)
    xp_ref[pad:pad + t_len, :] = x
    y = cb_ref[...] + cw_ref[CONV_WIDTH - 1:CONV_WIDTH, :] * x
    for j in range(CONV_WIDTH - 1):
        back = CONV_WIDTH - 1 - j
        y = y + cw_ref[j:j + 1, :] * xp_ref[pad - back:pad - back + t_len, :]
    xp_ref[0:pad, :] = x[t_len - pad:, :]

    y16 = y.astype(BF16)
    yh = 0.5 * y
    z = -lam_ref[...]
    softplus = jnp.maximum(z, 0.0) + jnp.log1p(jnp.exp(-jnp.abs(z)))
    k = (-0.5 * LRU_C * LOG2E) * softplus
    for j in range(N_RNN_TILES):
        ls = slice(j * LANES, (j + 1) * LANES)
        w0 = _gate_window_start(j) * LANES
        g = jnp.dot(y16[:, w0:w0 + GATE_WIN * LANES], wg_ref[j],
                    preferred_element_type=F32) + bg_ref[j]
        t_r = jnp.tanh(g[:, :LANES])
        t_i = jnp.tanh(g[:, LANES:])
        a = jnp.exp2(k[:, ls] * t_r + k[:, ls])
        a_ref[:, ls] = a
        u_ref[:, ls] = jnp.sqrt(1.0 - a * a) * (t_i * yh[:, ls] + yh[:, ls])

    def step(t, h):
        h = a_ref[pl.ds(t, 1), :] * h + u_ref[pl.ds(t, 1), :]
        h_ref[pl.ds(t, 1), :] = h
        return h

    hc_ref[...] = lax.fori_loop(0, t_len, step, hc_ref[...], unroll=8)
    gh = 0.5 * gr_ref[...].astype(F32)
    o_ref[...] = (h_ref[...] * (gh * jnp.tanh(gh) + gh)).astype(o_ref.dtype)


def _rglru(pc, b, s, conv_w, conv_b, w_gates, b_gates, lam):
    pc3 = pc.reshape(b, s, pc.shape[1])
    t = RNN_T
    const = lambda bi, c: (0, 0)
    const3 = lambda bi, c: (0, 0, 0)
    return pl.pallas_call(
        _rglru_kernel,
        grid=(b, s // t),
        in_specs=[
            pl.BlockSpec((None, t, D_RNN), lambda bi, c: (bi, c, 0)),
            pl.BlockSpec((None, t, D_RNN), lambda bi, c: (bi, c, 1)),
            pl.BlockSpec((CONV_WIDTH, D_RNN), const),
            pl.BlockSpec((1, D_RNN), const),
            pl.BlockSpec((N_RNN_TILES, GATE_WIN * LANES, 2 * LANES), const3),
            pl.BlockSpec((N_RNN_TILES, 1, 2 * LANES), const3),
            pl.BlockSpec((1, D_RNN), const),
        ],
        out_specs=pl.BlockSpec((None, t, D_RNN), lambda bi, c: (bi, c, 0)),
        out_shape=jax.ShapeDtypeStruct((b, s, D_RNN), BF16),
        scratch_shapes=[pltpu.VMEM((t + SUBLANES, D_RNN), F32),
                        pltpu.VMEM((t, D_RNN), F32),
                        pltpu.VMEM((t, D_RNN), F32),
                        pltpu.VMEM((t, D_RNN), F32),
                        pltpu.VMEM((1, D_RNN), F32)],
        compiler_params=_cparams(2),
        name="rglru",
    )(pc3, pc3, conv_w, conv_b.reshape(1, D_RNN), w_gates, b_gates,
      lam.reshape(1, D_RNN))


def _pack_gates(w_rg_l, b_rg_l, w_ig_l, b_ig_l):
    eye = jnp.eye(RNN_BLOCKS, dtype=w_rg_l.dtype)[:, None, :, None]
    dense = [(w[:, :, None, :] * eye).reshape(D_RNN, D_RNN) for w in (w_rg_l, w_ig_l)]
    bands, biases = [], []
    for j in range(N_RNN_TILES):
        w0 = _gate_window_start(j) * LANES
        cols = slice(j * LANES, (j + 1) * LANES)
        bands.append(jnp.concatenate(
            [dm[w0:w0 + GATE_WIN * LANES, cols] for dm in dense], axis=1))
        biases.append(jnp.concatenate([b_rg_l[cols], b_ig_l[cols]])[None, :])
    return (0.5 * jnp.stack(bands)).astype(BF16), 0.5 * jnp.stack(biases)


def _merge_kernel(at_ref, hr_ref, ma_ref, mb_ref, x_ref, woa_ref, wor_ref,
                  wo_ref, g_ref, *out_refs):
    y_a = jnp.dot(at_ref[...], woa_ref[...], preferred_element_type=F32)
    y_b = jnp.dot(hr_ref[...], wor_ref[...], preferred_element_type=F32)
    merged = (_sigmoid(ma_ref[...].astype(F32)) * y_a
              + _sigmoid(mb_ref[...].astype(F32)) * y_b)
    x = x_ref[...] + jnp.dot(merged.astype(BF16), wo_ref[...],
                             preferred_element_type=F32)
    no_ref = out_refs[-1]
    if len(out_refs) == 2:
        out_refs[0][...] = x
    y = x * lax.rsqrt(jnp.mean(x * x, axis=-1, keepdims=True) + NORM_EPS)
    no_ref[...] = (y * g_ref[...]).astype(no_ref.dtype)


def _merge(attn_g, hr_g, pv, x2d, w_oa, w_or, w_o, g, last):
    m, d = x2d.shape
    tm = 512
    const = lambda i: (0, 0)
    row = pl.BlockSpec((tm, d), lambda i: (i, 0))
    normed = jax.ShapeDtypeStruct((m, d), F32 if last else BF16)
    outs = pl.pallas_call(
        _merge_kernel,
        grid=(m // tm,),
        in_specs=[
            pl.BlockSpec((tm, ATTN_WIDTH), lambda i: (i, 0)),
            pl.BlockSpec((tm, D_RNN), lambda i: (i, 0)),
            pl.BlockSpec((tm, d), lambda i: (i, 1)),
            pl.BlockSpec((tm, d), lambda i: (i, 2)),
            row,
            pl.BlockSpec((ATTN_WIDTH, d), const),
            pl.BlockSpec((D_RNN, d), const),
            pl.BlockSpec((d, d), const),
            pl.BlockSpec((1, d), const),
        ],
        out_specs=[row] if last else [row, row],
        out_shape=[normed] if last else [jax.ShapeDtypeStruct((m, d), F32), normed],
        compiler_params=_cparams(1),
        name="merge",
    )(attn_g, hr_g, pv, pv, x2d, w_oa, w_or, w_o, g.reshape(1, d))
    return (None, outs[0]) if last else tuple(outs)


def _pack_weights(w_in_l):
    offs = np.cumsum((ATTN_WIDTH,) * 4 + (IDX_HEADS * IDX_DIM, IDX_DIM, IDX_HEADS,
                                          D_RNN, D_RNN, D_MODEL, D_MODEL))[:-1]
    w16 = w_in_l.astype(BF16)
    q, k, v, ga, iq, ik, iw, xr, gr, ma, mb = jnp.split(w16, offs.tolist(), axis=1)
    d = w_in_l.shape[0]
    half = IDX_DIM // 2
    z = lambda n: jnp.zeros((d, n), BF16)
    iq = iq.reshape(d, IDX_HEADS // 2, 2, 2, half).transpose(0, 1, 3, 2, 4).reshape(
        d, IDX_HEADS * IDX_DIM)
    ik1, ik2 = ik[:, :half], ik[:, half:]
    tail = jnp.concatenate([ik1, z(half), ik2, z(half), z(half), ik1, z(half), ik2,
                            iw, z(ROPE_TN - 2 * LANES - IDX_HEADS)], axis=1)
    w_rope = jnp.concatenate([q, k, iq, tail], axis=1)
    w_gate = jnp.concatenate([ga, ma, mb], axis=1)
    vt = v.T.reshape(N_HEADS, HEAD_DIM, d)
    w_vt = jnp.pad(vt, ((0, 0), (0, V_ROWS - HEAD_DIM), (0, 0))).reshape(
        N_HEADS * V_ROWS, d)
    w_rnn = jnp.concatenate([xr, gr], axis=1)
    return w_rope, w_gate, w_vt, w_rnn


def kernel(x, positions, norm_g, w_in, conv_w, conv_b, w_rg, b_rg, w_ig, b_ig,
           lru_lambda, w_out_attn, w_out_rnn, w_o, final_g):
    b, s, d = x.shape
    depth = w_in.shape[0]
    m = b * s
    cos, sin = _rope_tables(positions)
    x2d = x.reshape(m, d)
    h = _rmsnorm(x2d, norm_g[0], BF16)
    out = None
    for l in range(depth):
        w_rope, w_gate, w_vt, w_rnn = _pack_weights(w_in[l])
        w_gates, b_gates = _pack_gates(w_rg[l], b_rg[l], w_ig[l], b_ig[l])
        pr = _proj_rope(h, w_rope, cos, sin)
        pv = _proj_plain(h, w_gate, 1024, "proj_gate")
        vt = _proj_vt(h, w_vt, b, s)
        pc = _proj_plain(h, w_rnn, D_RNN, "proj_rnn")
        attn_g = _dsa_attention(pr, pv, vt, b, s).reshape(m, ATTN_WIDTH)
        hr_g = _rglru(pc, b, s, conv_w[l], conv_b[l], w_gates, b_gates,
                      lru_lambda[l]).reshape(m, D_RNN)
        last = l == depth - 1
        g_next = final_g if last else norm_g[l + 1]
        x2d, normed = _merge(attn_g, hr_g, pv, x2d,
                             w_out_attn[l].astype(BF16), w_out_rnn[l].astype(BF16),
                             w_o[l].astype(BF16), g_next, last)
        if last:
            out = normed
        else:
            h = normed
    return out.reshape(b, s, d)
```

```python
import functools

import jax
import jax.numpy as jnp
import numpy as np
from jax import lax
from jax.experimental import pallas as pl
from jax.experimental.pallas import tpu as pltpu

D_MODEL = 1024
N_HEADS = 8
HEAD_DIM = 128
ATTN_WIDTH = N_HEADS * HEAD_DIM
IDX_HEADS = 16
IDX_DIM = 64
TOPK_MAX = 256
Q_BLOCK = 256
D_RNN = 1408
RNN_BLOCKS = 16
RNN_BLOCK_DIM = D_RNN // RNN_BLOCKS
CONV_WIDTH = 4
LRU_C = 8.0
ROPE_THETA = 10000.0
NORM_EPS = 1e-6

LANES = 128
SUBLANES = 8
PACKED_SUBLANES = 16
VMEM_LIMIT = 56 * 1024 * 1024

F32 = jnp.float32
BF16 = jnp.bfloat16
NEG_INF = float("-inf")
NT_DIMS = (((1,), (1,)), ((), ()))
LOG2E = float(np.log2(np.e))
V_ROWS = HEAD_DIM + SUBLANES


def _sigmoid(x):
    return 0.5 * jnp.tanh(0.5 * x) + 0.5


def _cparams(n_grid):
    return pltpu.CompilerParams(
        dimension_semantics=("arbitrary",) * n_grid,
        vmem_limit_bytes=VMEM_LIMIT)


def _rope_table_kernel(pos_ref, inv_ref, cos_ref, sin_ref):
    ang = pos_ref[...].astype(F32) * inv_ref[...]
    cos, sin = jnp.cos(ang), jnp.sin(ang)
    cos_r, sin_r = pltpu.roll(cos, LANES // 2, 1), pltpu.roll(sin, LANES // 2, 1)
    low = lax.broadcasted_iota(jnp.int32, (1, LANES), 1) < LANES // 2
    cos_ref[:, :LANES] = jnp.where(low, cos, cos_r)
    cos_ref[:, LANES:] = jnp.where(low, cos_r, cos)
    sin_ref[:, :LANES] = jnp.where(low, -sin, sin_r)
    sin_ref[:, LANES:] = jnp.where(low, -sin_r, sin)


def _rope_tables(positions):
    m = positions.size
    inv_a = ROPE_THETA ** (-jnp.arange(0, HEAD_DIM, 2, dtype=F32) / HEAD_DIM)
    inv_i = ROPE_THETA ** (-jnp.arange(0, IDX_DIM, 2, dtype=F32) / IDX_DIM)
    inv = jnp.concatenate([inv_a, inv_i, inv_i])[None, :]
    tm = 2048
    return pl.pallas_call(
        _rope_table_kernel,
        grid=(m // tm,),
        in_specs=[pl.BlockSpec((tm, 1), lambda i: (i, 0)),
                  pl.BlockSpec((1, LANES), lambda i: (0, 0))],
        out_specs=[pl.BlockSpec((tm, 2 * LANES), lambda i: (i, 0))] * 2,
        out_shape=[jax.ShapeDtypeStruct((m, 2 * LANES), F32)] * 2,
        compiler_params=_cparams(1),
        name="rope_tables",
    )(positions.reshape(m, 1), inv)


def _rmsnorm_kernel(x_ref, g_ref, o_ref):
    x = x_ref[...]
    y = x * lax.rsqrt(jnp.mean(x * x, axis=-1, keepdims=True) + NORM_EPS)
    o_ref[...] = (y * g_ref[...]).astype(o_ref.dtype)


def _rmsnorm(x2d, g, out_dtype):
    m, d = x2d.shape
    tm = 1024
    return pl.pallas_call(
        _rmsnorm_kernel,
        grid=(m // tm,),
        in_specs=[pl.BlockSpec((tm, d), lambda i: (i, 0)),
                  pl.BlockSpec((1, d), lambda i: (0, 0))],
        out_specs=pl.BlockSpec((tm, d), lambda i: (i, 0)),
        out_shape=jax.ShapeDtypeStruct((m, d), out_dtype),
        compiler_params=_cparams(1),
        name="rmsnorm",
    )(x2d, g.reshape(1, d))


ROPE_TN = 512


ROPE_TILES = 7


def _rope_epilogue(acc, jp, cos_ref, sin_ref, o_ref):
    cos_a, sin_a = cos_ref[:, :LANES], sin_ref[:, :LANES]
    cos_i, sin_i = cos_ref[:, LANES:], sin_ref[:, LANES:]
    q_scale = (HEAD_DIM ** -0.5) * LOG2E
    use_a = jnp.where(jp < 2, q_scale, jnp.where(jp < 4, 1.0, 0.0)).astype(F32)
    idx_tile = jnp.logical_and(jp >= 4, jp < 6)
    tail = jp == ROPE_TILES - 1
    for c in range(ROPE_TN // LANES):
        ik_lanes = c < 2
        use_i = jnp.where(jnp.logical_or(idx_tile, jnp.logical_and(tail, ik_lanes)),
                          1.0, 0.0).astype(F32)
        ident = jnp.where(jnp.logical_and(tail, not ik_lanes), 1.0, 0.0).astype(F32)
        cos = cos_a * use_a + cos_i * use_i + ident
        sin = sin_a * use_a + sin_i * use_i
        x = acc[:, c * LANES:(c + 1) * LANES]
        o_ref[:, c * LANES:(c + 1) * LANES] = (
            x * cos + pltpu.roll(x, LANES // 2, 1) * sin).astype(o_ref.dtype)


def _proj_rope_kernel(h_ref, w_ref, cos_ref, sin_ref, o_ref, acc0_ref, acc1_ref):
    step = pl.program_id(0)
    jp = jnp.maximum(step - 1, 0) % ROPE_TILES

    @pl.when(step == 0)
    def _():
        acc1_ref[...] = jnp.zeros(acc1_ref.shape, F32)

    def body(acc_new, acc_old):
        acc_new[...] = jnp.dot(h_ref[...], w_ref[...], preferred_element_type=F32)
        _rope_epilogue(acc_old[...], jp, cos_ref, sin_ref, o_ref)

    @pl.when(step % 2 == 0)
    def _():
        body(acc0_ref, acc1_ref)

    @pl.when(step % 2 == 1)
    def _():
        body(acc1_ref, acc0_ref)


def _proj_rope(h, w, cos, sin):
    m, d = h.shape
    n = w.shape[1]
    assert n == ROPE_TILES * ROPE_TN
    tm = 1024
    n_steps = (m // tm) * ROPE_TILES
    cur = lambda s: jnp.minimum(s, n_steps - 1)
    prev = lambda s: jnp.maximum(s - 1, 0)
    return pl.pallas_call(
        _proj_rope_kernel,
        grid=(n_steps + 1,),
        in_specs=[pl.BlockSpec((tm, d), lambda s: (cur(s) // ROPE_TILES, 0)),
                  pl.BlockSpec((d, ROPE_TN), lambda s: (0, cur(s) % ROPE_TILES)),
                  pl.BlockSpec((tm, 2 * LANES), lambda s: (prev(s) // ROPE_TILES, 0)),
                  pl.BlockSpec((tm, 2 * LANES), lambda s: (prev(s) // ROPE_TILES, 0))],
        out_specs=pl.BlockSpec(
            (tm, ROPE_TN), lambda s: (prev(s) // ROPE_TILES, prev(s) % ROPE_TILES)),
        out_shape=jax.ShapeDtypeStruct((m, n), BF16),
        scratch_shapes=[pltpu.VMEM((tm, ROPE_TN), F32)] * 2,
        compiler_params=_cparams(1),
        name="proj_rope",
    )(h, w, cos, sin)


def _proj_plain_kernel(h_ref, w_ref, o_ref):
    o_ref[...] = jnp.dot(h_ref[...], w_ref[...],
                         preferred_element_type=F32).astype(o_ref.dtype)


def _proj_plain(h, w, tn, name):
    m, d = h.shape
    n = w.shape[1]
    tm = 1024
    return pl.pallas_call(
        _proj_plain_kernel,
        grid=(m // tm, n // tn),
        in_specs=[pl.BlockSpec((tm, d), lambda i, j: (i, 0)),
                  pl.BlockSpec((d, tn), lambda i, j: (0, j))],
        out_specs=pl.BlockSpec((tm, tn), lambda i, j: (i, j)),
        out_shape=jax.ShapeDtypeStruct((m, n), BF16),
        compiler_params=_cparams(2),
        name=name,
    )(h, w)


ATT_TK = 256


VT_TILES = 2


def _proj_vt_kernel(h_ref, wt_ref, o_ref):
    vt = lax.dot_general(wt_ref[...], h_ref[...], NT_DIMS,
                         preferred_element_type=F32)
    row = lax.broadcasted_iota(jnp.int32, (vt.shape[0], 1), 0)
    vt = jnp.where(row % V_ROWS == HEAD_DIM, 1.0, vt).astype(o_ref.dtype)
    for j in range(VT_TILES):
        o_ref[j] = vt[:, j * ATT_TK:(j + 1) * ATT_TK]


def _proj_vt(h, wt, b, s):
    d = h.shape[1]
    n = wt.shape[0]
    tk = ATT_TK
    rows = VT_TILES * tk
    return pl.pallas_call(
        _proj_vt_kernel,
        grid=(b, s // rows),
        in_specs=[pl.BlockSpec((None, rows, d), lambda bi, t: (bi, t, 0)),
                  pl.BlockSpec((n, d), lambda bi, t: (0, 0))],
        out_specs=pl.BlockSpec((None, VT_TILES, n, tk), lambda bi, t: (bi, t, 0, 0)),
        out_shape=jax.ShapeDtypeStruct((b, s // tk, n, tk), BF16),
        compiler_params=_cparams(2),
        name="proj_vt",
    )(h.reshape(b, s, d), wt)


SCORE_TK = 512
COARSE_BITS = 16
FINE_UNCHECKED = 10
N_CHAINS = 4
M_INIT = -1e30


def _key16_to_f32(u):
    key = u ^ 0x8000
    bits = jnp.where((key & 0x8000) != 0, key ^ 0x7FFF, key)
    return pltpu.bitcast(jnp.left_shift(bits, 16), F32)


def _f32_to_key32(x):
    bits = pltpu.bitcast(x, jnp.int32)
    return jnp.where(bits < 0, bits ^ 0x7FFFFFFF, bits)


def _key32_to_f32(key):
    return pltpu.bitcast(jnp.where(key < 0, key ^ 0x7FFFFFFF, key), F32)


def _dsa_kernel(q_ref, iq_ref, k_ref, ikw_ref, vt_ref, ga_ref, o_ref,
                sc_ref, sc16_ref, bias_ref, acc_ref, p_ref, *, topk):
    i = pl.program_id(1)
    tk = ATT_TK
    nk = ((i + 1) * Q_BLOCK + tk - 1) // tk
    q0 = pl.multiple_of(i * Q_BLOCK, Q_BLOCK)
    qidx = q0 + lax.broadcasted_iota(jnp.int32, (1, Q_BLOCK), 1)

    def tile_start(t):
        return pl.multiple_of(t * tk, tk)

    def key_index(k0):
        return k0 + lax.broadcasted_iota(jnp.int32, (tk, 1), 0)

    iq = iq_ref[...]
    n_pair = IDX_HEADS // 2
    lhs = jnp.concatenate(
        [iq[:, p * LANES:(p + 1) * LANES] for p in range(n_pair)], axis=0)
    w = ikw_ref[pl.ds(q0, Q_BLOCK), 2 * LANES:3 * LANES].astype(F32)
    wt = w.T * ((IDX_HEADS ** -0.5) * (IDX_DIM ** -0.5))
    wrow = [wt[h:h + 1, :] for h in range(IDX_HEADS)]

    def score_tile(t, carry):
        k0 = pl.multiple_of(t * SCORE_TK, SCORE_TK)
        acc = jnp.zeros((SCORE_TK, Q_BLOCK), F32)
        for e in range(2):
            ik_e = ikw_ref[pl.ds(k0, SCORE_TK), e * LANES:(e + 1) * LANES]
            d = lax.dot_general(ik_e, lhs, NT_DIMS,
                                preferred_element_type=F32)
            for p in range(n_pair):
                dp = d[:, p * Q_BLOCK:(p + 1) * Q_BLOCK]
                acc = acc + wrow[2 * p + e] * jnp.maximum(dp, 0.0)
        kidx = k0 + lax.broadcasted_iota(jnp.int32, (SCORE_TK, 1), 0)
        sc = jnp.where(kidx <= qidx, acc, NEG_INF)
        sc_ref[pl.ds(k0, SCORE_TK), :] = sc
        sc16_ref[pl.ds(k0, SCORE_TK), :] = sc.astype(BF16)
        return carry

    lax.fori_loop(0, ((i + 1) * Q_BLOCK + SCORE_TK - 1) // SCORE_TK, score_tile, 0)

    one16 = jnp.ones((), BF16)
    zero16 = jnp.zeros((), BF16)
    n_packed = tk // PACKED_SUBLANES

    def coarse(it, t_u):
        cand_u = t_u | jnp.left_shift(jnp.int32(1), COARSE_BITS - 1 - it)
        cand = _key16_to_f32(cand_u).astype(BF16)

        def count_tile(t, cnts):
            x = sc16_ref[pl.ds(tile_start(t), tk), :]
            ind = jnp.where(x >= cand, one16, zero16)
            cnts = list(cnts)
            for j in range(n_packed):
                cnts[j % N_CHAINS] = cnts[j % N_CHAINS] + ind[
                    j * PACKED_SUBLANES:(j + 1) * PACKED_SUBLANES, :]
            return tuple(cnts)

        cnts = lax.fori_loop(
            0, nk, count_tile,
            (jnp.zeros((PACKED_SUBLANES, Q_BLOCK), BF16),) * N_CHAINS)
        total = jnp.sum(sum(c.astype(F32) for c in cnts), axis=0, keepdims=True)
        return jnp.where(total >= topk, cand_u, t_u)

    t_u = lax.fori_loop(0, COARSE_BITS, coarse,
                        jnp.zeros((1, Q_BLOCK), jnp.int32))

    def count_ge(cand):
        def count_tile(t, cnts):
            s = sc_ref[pl.ds(tile_start(t), tk), :]
            ind = jnp.where(s >= cand, 1.0, 0.0)
            cnts = list(cnts)
            for j in range(tk // SUBLANES):
                cnts[j % N_CHAINS] = cnts[j % N_CHAINS] + ind[
                    j * SUBLANES:(j + 1) * SUBLANES, :]
            return tuple(cnts)

        cnts = lax.fori_loop(
            0, nk, count_tile,
            (jnp.zeros((SUBLANES, Q_BLOCK), F32),) * N_CHAINS)
        return jnp.sum(sum(cnts), axis=0, keepdims=True)

    base = _f32_to_key32(_key16_to_f32(t_u))
    lo0 = base - 0x8000
    hi0 = base + 0x18001

    def fine_cond(st):
        lo, hi, cnt_lo = st
        open_ = jnp.logical_and(cnt_lo != topk, hi - lo > 1)
        return jnp.max(jnp.where(open_, 1.0, 0.0)) > 0.0

    def fine_step(st):
        lo, hi, cnt_lo = st
        mid = lo + jnp.right_shift(hi - lo, 1)
        cnt = count_ge(_key32_to_f32(mid))
        ge = cnt >= topk
        return (jnp.where(ge, mid, lo), jnp.where(ge, hi, mid),
                jnp.where(ge, cnt, cnt_lo))

    st = (lo0, hi0, jnp.full((1, Q_BLOCK), -1.0, F32))
    st = lax.fori_loop(0, FINE_UNCHECKED, lambda _, s_: fine_step(s_), st)
    lo, _, cnt_lo = lax.while_loop(
        fine_cond, lambda s_: fine_step(fine_step(s_)), st)
    full_row = qidx < topk
    thr = jnp.where(full_row, NEG_INF, _key32_to_f32(lo))

    def causal_bias(k0):
        return jnp.where(key_index(k0) <= qidx, 0.0, NEG_INF)

    def write_bias():
        def bias_tile(t, carry):
            k0 = tile_start(t)
            s = sc_ref[pl.ds(k0, tk), :]
            bias_ref[pl.ds(k0, tk), :] = jnp.where(s >= thr, causal_bias(k0), NEG_INF)
            return carry

        lax.fori_loop(0, nk, bias_tile, 0)

    def write_bias_with_ties():
        def count_where(pred):
            def count_tile(t, cnt):
                k0 = tile_start(t)
                ind = jnp.where(pred(sc_ref[pl.ds(k0, tk), :], key_index(k0)), 1.0, 0.0)
                return cnt + jnp.sum(ind, axis=0, keepdims=True)

            return lax.fori_loop(0, nk, count_tile, jnp.zeros((1, Q_BLOCK), F32))

        n_tied = topk - count_where(lambda s, kidx: s > thr)

        def index_step(_, st_):
            lo_i, hi_i = st_
            mid = lo_i + jnp.right_shift(hi_i - lo_i, 1)
            cnt = count_where(
                lambda s, kidx: jnp.logical_and(s == thr, kidx <= mid))
            ge = cnt >= n_tied
            return jnp.where(ge, lo_i, mid), jnp.where(ge, mid, hi_i)

        n_keys = nk * tk
        steps = int(np.ceil(np.log2(sc_ref.shape[0]))) + 1
        _, last = lax.fori_loop(
            0, steps, index_step,
            (jnp.full((1, Q_BLOCK), -1, jnp.int32),
             jnp.full((1, Q_BLOCK), 1, jnp.int32) * (n_keys - 1)))

        def bias_tile(t, carry):
            k0 = tile_start(t)
            s = sc_ref[pl.ds(k0, tk), :]
            keep = jnp.logical_or(
                s > thr, jnp.logical_and(s == thr, key_index(k0) <= last))
            bias_ref[pl.ds(k0, tk), :] = jnp.where(keep, causal_bias(k0), NEG_INF)
            return carry

        lax.fori_loop(0, nk, bias_tile, 0)

    tied = jnp.logical_and(cnt_lo != topk, jnp.logical_not(full_row))
    lax.cond(jnp.max(jnp.where(tied, 1.0, 0.0)) > 0.0,
             write_bias_with_ties, write_bias)

    acc_ref[...] = jnp.zeros(acc_ref.shape, F32)

    def qk_head(t, h, m_old):
        k0 = tile_start(t)
        hs = slice(h * HEAD_DIM, (h + 1) * HEAD_DIM)
        s = lax.dot_general(k_ref[pl.ds(k0, tk), hs], q_ref[:, hs], NT_DIMS,
                            preferred_element_type=F32)
        s = s + bias_ref[pl.ds(k0, tk), :]
        m_new = jnp.maximum(m_old, jnp.max(s, axis=0, keepdims=True))
        p_ref[t % 2, h] = jnp.exp2(s - m_new).astype(BF16)
        return m_new, jnp.exp2(m_old - m_new)

    def pv_head(t, h, alpha):
        pv = jnp.dot(vt_ref[t, h * V_ROWS:(h + 1) * V_ROWS, :], p_ref[t % 2, h],
                     preferred_element_type=F32)
        acc_ref[h] = acc_ref[h] * alpha + pv

    def qk_stage(t, ms):
        pairs = [qk_head(t, h, ms[h]) for h in range(N_HEADS)]
        return tuple(p[0] for p in pairs), tuple(p[1] for p in pairs)

    def pv_stage(t, alphas):
        for h in range(N_HEADS):
            pv_head(t, h, alphas[h])

    def att_tile(t, carry):
        ms, alphas = carry
        pairs = []
        for h in range(N_HEADS):
            pv_head(t - 1, h, alphas[h])
            pairs.append(qk_head(t, h, ms[h]))
        return tuple(p[0] for p in pairs), tuple(p[1] for p in pairs)

    m0 = tuple(jnp.full((1, Q_BLOCK), M_INIT, F32) for _ in range(N_HEADS))
    carry = lax.fori_loop(
        0, (nk - 1) // 2,
        lambda j, c: att_tile(2 * j + 2, att_tile(2 * j + 1, c)),
        qk_stage(0, m0))
    carry = lax.cond((nk - 1) % 2 == 1,
                     lambda c: att_tile(nk - 1, c), lambda c: c, carry)
    pv_stage(nk - 1, carry[1])

    for h in range(N_HEADS):
        hs = slice(h * HEAD_DIM, (h + 1) * HEAD_DIM)
        out_t = acc_ref[h, :HEAD_DIM, :] * (1.0 / acc_ref[h, HEAD_DIM:HEAD_DIM + 1, :])
        gate = ga_ref[:, hs].astype(F32)
        o_ref[:, hs] = (out_t.T * (gate * _sigmoid(gate))).astype(o_ref.dtype)


def _dsa_attention(pr, pv, vt, b, s):
    pr3 = pr.reshape(b, s, pr.shape[1])
    pv3 = pv.reshape(b, s, pv.shape[1])
    topk = min(TOPK_MAX, s // 4)
    w_att = ATTN_WIDTH
    single = pl.Buffered(1)
    return pl.pallas_call(
        functools.partial(_dsa_kernel, topk=topk),
        grid=(b, s // Q_BLOCK),
        in_specs=[
            pl.BlockSpec((None, Q_BLOCK, w_att), lambda bi, i: (bi, i, 0)),
            pl.BlockSpec((None, Q_BLOCK, w_att), lambda bi, i: (bi, i, 2)),
            pl.BlockSpec((None, s, w_att), lambda bi, i: (bi, 0, 1),
                         pipeline_mode=single),
            pl.BlockSpec((None, s, 3 * LANES), lambda bi, i: (bi, 0, 8),
                         pipeline_mode=single),
            pl.BlockSpec((None, s // ATT_TK, N_HEADS * V_ROWS, ATT_TK),
                         lambda bi, i: (bi, 0, 0, 0), pipeline_mode=single),
            pl.BlockSpec((None, Q_BLOCK, w_att), lambda bi, i: (bi, i, 0)),
        ],
        out_specs=pl.BlockSpec((None, Q_BLOCK, w_att), lambda bi, i: (bi, i, 0)),
        out_shape=jax.ShapeDtypeStruct((b, s, w_att), BF16),
        scratch_shapes=[pltpu.VMEM((s, Q_BLOCK), F32),
                        pltpu.VMEM((s, Q_BLOCK), BF16),
                        pltpu.VMEM((s, Q_BLOCK), F32),
                        pltpu.VMEM((N_HEADS, V_ROWS, Q_BLOCK), F32),
                        pltpu.VMEM((2, N_HEADS, ATT_TK, Q_BLOCK), BF16)],
        compiler_params=_cparams(2),
        name="dsa_attention",
    )(pr3, pr3, pr3, pr3, vt, pv3)


RNN_T = 512
N_RNN_TILES = D_RNN // LANES
GATE_WIN = 3


def _gate_window_start(j):
    assert RNN_BLOCK_DIM <= LANES
    return min(max(j - 1, 0), N_RNN_TILES - GATE_WIN)


def _rglru_kernel(xr_ref, gr_ref, cw_ref, cb_ref, wg_ref, bg_ref, lam_ref,
                  o_ref, xp_ref, a_ref, u_ref, h_ref, hc_ref):
    c = pl.program_id(1)
    t_len = RNN_T
    pad = SUBLANES

    @pl.when(c == 0)
    def _():
        xp_ref[0:pad, :] = jnp.zeros((pad, D_RNN), F32)
        hc_ref[...] = jnp.zeros((1, D_RNN), F32)

    x = xr_ref[...].astype(F32)
    xp_ref[pad:pad + t_len, :] = x
    y = cb_ref[...] + cw_ref[CONV_WIDTH - 1:CONV_WIDTH, :] * x
    for j in range(CONV_WIDTH - 1):
        back = CONV_WIDTH - 1 - j
        y = y + cw_ref[j:j + 1, :] * xp_ref[pad - back:pad - back + t_len, :]
    xp_ref[0:pad, :] = x[t_len - pad:, :]

    y16 = y.astype(BF16)
    yh = 0.5 * y
    z = -lam_ref[...]
    softplus = jnp.maximum(z, 0.0) + jnp.log1p(jnp.exp(-jnp.abs(z)))
    k = (-0.5 * LRU_C * LOG2E) * softplus
    for j in range(N_RNN_TILES):
        ls = slice(j * LANES, (j + 1) * LANES)
        w0 = _gate_window_start(j) * LANES
        g = jnp.dot(y16[:, w0:w0 + GATE_WIN * LANES], wg_ref[j],
                    preferred_element_type=F32) + bg_ref[j]
        t_r = jnp.tanh(g[:, :LANES])
        t_i = jnp.tanh(g[:, LANES:])
        a = jnp.exp2(k[:, ls] * t_r + k[:, ls])
        a_ref[:, ls] = a
        om = 1.0 - a * a
        root = jnp.where(om > 0.0, om * lax.rsqrt(om), 0.0)
        u_ref[:, ls] = root * (t_i * yh[:, ls] + yh[:, ls])

    def step(t, h):
        h = a_ref[pl.ds(t, 1), :] * h + u_ref[pl.ds(t, 1), :]
        h_ref[pl.ds(t, 1), :] = h
        return h

    hc_ref[...] = lax.fori_loop(0, t_len, step, hc_ref[...], unroll=8)
    gh = 0.5 * gr_ref[...].astype(F32)
    o_ref[...] = (h_ref[...] * (gh * jnp.tanh(gh) + gh)).astype(o_ref.dtype)


def _rglru(pc, b, s, conv_w, conv_b, w_gates, b_gates, lam):
    pc3 = pc.reshape(b, s, pc.shape[1])
    t = RNN_T
    const = lambda bi, c: (0, 0)
    const3 = lambda bi, c: (0, 0, 0)
    return pl.pallas_call(
        _rglru_kernel,
        grid=(b, s // t),
        in_specs=[
            pl.BlockSpec((None, t, D_RNN), lambda bi, c: (bi, c, 0)),
            pl.BlockSpec((None, t, D_RNN), lambda bi, c: (bi, c, 1)),
            pl.BlockSpec((CONV_WIDTH, D_RNN), const),
            pl.BlockSpec((1, D_RNN), const),
            pl.BlockSpec((N_RNN_TILES, GATE_WIN * LANES, 2 * LANES), const3),
            pl.BlockSpec((N_RNN_TILES, 1, 2 * LANES), const3),
            pl.BlockSpec((1, D_RNN), const),
        ],
        out_specs=pl.BlockSpec((None, t, D_RNN), lambda bi, c: (bi, c, 0)),
        out_shape=jax.ShapeDtypeStruct((b, s, D_RNN), BF16),
        scratch_shapes=[pltpu.VMEM((t + SUBLANES, D_RNN), F32),
                        pltpu.VMEM((t, D_RNN), F32),
                        pltpu.VMEM((t, D_RNN), F32),
                        pltpu.VMEM((t, D_RNN), F32),
                        pltpu.VMEM((1, D_RNN), F32)],
        compiler_params=_cparams(2),
        name="rglru",
    )(pc3, pc3, conv_w, conv_b.reshape(1, D_RNN), w_gates, b_gates,
      lam.reshape(1, D_RNN))


def _pack_gates(w_rg_l, b_rg_l, w_ig_l, b_ig_l):
    dense = [jax.scipy.linalg.block_diag(*w) for w in (w_rg_l, w_ig_l)]
    bands, biases = [], []
    for j in range(N_RNN_TILES):
        w0 = _gate_window_start(j) * LANES
        cols = slice(j * LANES, (j + 1) * LANES)
        bands.append(jnp.concatenate(
            [dm[w0:w0 + GATE_WIN * LANES, cols] for dm in dense], axis=1))
        biases.append(jnp.concatenate([b_rg_l[cols], b_ig_l[cols]])[None, :])
    return (0.5 * jnp.stack(bands)).astype(BF16), 0.5 * jnp.stack(biases)


def _merge_kernel(at_ref, hr_ref, ma_ref, mb_ref, x_ref, woa_ref, wor_ref,
                  wo_ref, g_ref, *out_refs):
    y_a = jnp.dot(at_ref[...], woa_ref[...], preferred_element_type=F32)
    y_b = jnp.dot(hr_ref[...], wor_ref[...], preferred_element_type=F32)
    merged = (_sigmoid(ma_ref[...].astype(F32)) * y_a
              + _sigmoid(mb_ref[...].astype(F32)) * y_b)
    x = x_ref[...] + jnp.dot(merged.astype(BF16), wo_ref[...],
                             preferred_element_type=F32)
    no_ref = out_refs[-1]
    if len(out_refs) == 2:
        out_refs[0][...] = x
    y = x * lax.rsqrt(jnp.mean(x * x, axis=-1, keepdims=True) + NORM_EPS)
    no_ref[...] = (y * g_ref[...]).astype(no_ref.dtype)


def _merge(attn_g, hr_g, pv, x2d, w_oa, w_or, w_o, g, last):
    m, d = x2d.shape
    tm = 512
    const = lambda i: (0, 0)
    row = pl.BlockSpec((tm, d), lambda i: (i, 0))
    normed = jax.ShapeDtypeStruct((m, d), F32 if last else BF16)
    outs = pl.pallas_call(
        _merge_kernel,
        grid=(m // tm,),
        in_specs=[
            pl.BlockSpec((tm, ATTN_WIDTH), lambda i: (i, 0)),
            pl.BlockSpec((tm, D_RNN), lambda i: (i, 0)),
            pl.BlockSpec((tm, d), lambda i: (i, 1)),
            pl.BlockSpec((tm, d), lambda i: (i, 2)),
            row,
            pl.BlockSpec((ATTN_WIDTH, d), const),
            pl.BlockSpec((D_RNN, d), const),
            pl.BlockSpec((d, d), const),
            pl.BlockSpec((1, d), const),
        ],
        out_specs=[row] if last else [row, row],
        out_shape=[normed] if last else [jax.ShapeDtypeStruct((m, d), F32), normed],
        compiler_params=_cparams(1),
        name="merge",
    )(attn_g, hr_g, pv, pv, x2d, w_oa, w_or, w_o, g.reshape(1, d))
    return (None, outs[0]) if last else tuple(outs)


def _pack_weights(w_in_l):
    sizes = (ATTN_WIDTH,) * 4 + (IDX_HEADS * IDX_DIM, IDX_DIM, IDX_HEADS,
                                 D_RNN, D_RNN, D_MODEL, D_MODEL)
    (o_q, o_k, o_v, o_ga, o_iq, o_ik, o_iw, o_xr, o_gr, o_ma,
     o_mb, o_end) = np.concatenate([[0], np.cumsum(sizes)]).tolist()
    w16 = w_in_l.astype(BF16)
    d = w_in_l.shape[0]
    half = IDX_DIM // 2
    z = lambda n: jnp.zeros((d, n), BF16)
    iq = w16[:, o_iq:o_ik].reshape(d, IDX_HEADS // 2, 2, 2, half).transpose(
        0, 1, 3, 2, 4).reshape(d, IDX_HEADS * IDX_DIM)
    ik1, ik2 = w16[:, o_ik:o_ik + half], w16[:, o_ik + half:o_iw]
    w_rope = jnp.concatenate(
        [w16[:, o_q:o_v], iq,
         ik1, z(half), ik2, z(half), z(half), ik1, z(half), ik2,
         w16[:, o_iw:o_xr], z(ROPE_TN - 2 * LANES - IDX_HEADS)], axis=1)
    w_gate = jnp.concatenate([w16[:, o_ga:o_iq], w16[:, o_ma:o_end]], axis=1)
    vt = w16[:, o_v:o_ga].T.reshape(N_HEADS, HEAD_DIM, d)
    w_vt = jnp.pad(vt, ((0, 0), (0, V_ROWS - HEAD_DIM), (0, 0))).reshape(
        N_HEADS * V_ROWS, d)
    w_rnn = w16[:, o_xr:o_ma]
    return w_rope, w_gate, w_vt, w_rnn


def kernel(x, positions, norm_g, w_in, conv_w, conv_b, w_rg, b_rg, w_ig, b_ig,
           lru_lambda, w_out_attn, w_out_rnn, w_o, final_g):
    b, s, d = x.shape
    depth = w_in.shape[0]
    m = b * s
    cos, sin = _rope_tables(positions)
    x2d = x.reshape(m, d)
    h = _rmsnorm(x2d, norm_g[0], BF16)
    out = None
    for l in range(depth):
        w_rope, w_gate, w_vt, w_rnn = _pack_weights(w_in[l])
        w_gates, b_gates = _pack_gates(w_rg[l], b_rg[l], w_ig[l], b_ig[l])
        pr = _proj_rope(h, w_rope, cos, sin)
        pv = _proj_plain(h, w_gate, 1024, "proj_gate")
        vt = _proj_vt(h, w_vt, b, s)
        pc = _proj_plain(h, w_rnn, D_RNN, "proj_rnn")
        attn_g = _dsa_attention(pr, pv, vt, b, s).reshape(m, ATTN_WIDTH)
        hr_g = _rglru(pc, b, s, conv_w[l], conv_b[l], w_gates, b_gates,
                      lru_lambda[l]).reshape(m, D_RNN)
        last = l == depth - 1
        g_next = final_g if last else norm_g[l + 1]
        x2d, normed = _merge(attn_g, hr_g, pv, x2d,
                             w_out_attn[l].astype(BF16), w_out_rnn[l].astype(BF16),
                             w_o[l].astype(BF16), g_next, last)
        if last:
            out = normed
        else:
            h = normed
    return out.reshape(b, s, d)
```

```python
import functools

import jax
import jax.numpy as jnp
import numpy as np
from jax import lax
from jax.experimental import pallas as pl
from jax.experimental.pallas import tpu as pltpu

D_MODEL = 1024
N_HEADS = 8
HEAD_DIM = 128
ATTN_WIDTH = N_HEADS * HEAD_DIM
IDX_HEADS = 16
IDX_DIM = 64
TOPK_MAX = 256
Q_BLOCK = 256
D_RNN = 1408
RNN_BLOCKS = 16
RNN_BLOCK_DIM = D_RNN // RNN_BLOCKS
CONV_WIDTH = 4
LRU_C = 8.0
ROPE_THETA = 10000.0
NORM_EPS = 1e-6

LANES = 128
SUBLANES = 8
PACKED_SUBLANES = 16
VMEM_LIMIT = 56 * 1024 * 1024

F32 = jnp.float32
BF16 = jnp.bfloat16
NEG_INF = float("-inf")
NT_DIMS = (((1,), (1,)), ((), ()))
LOG2E = float(np.log2(np.e))
V_ROWS = HEAD_DIM + SUBLANES


def _sigmoid(x):
    return 0.5 * jnp.tanh(0.5 * x) + 0.5


def _cparams(n_grid):
    return pltpu.CompilerParams(
        dimension_semantics=("arbitrary",) * n_grid,
        vmem_limit_bytes=VMEM_LIMIT)


def _rope_table_kernel(pos_ref, inv_ref, cos_ref, sin_ref):
    ang = pos_ref[...].astype(F32) * inv_ref[...]
    cos, sin = jnp.cos(ang), jnp.sin(ang)
    cos_r, sin_r = pltpu.roll(cos, LANES // 2, 1), pltpu.roll(sin, LANES // 2, 1)
    low = lax.broadcasted_iota(jnp.int32, (1, LANES), 1) < LANES // 2
    cos_ref[:, :LANES] = jnp.where(low, cos, cos_r)
    cos_ref[:, LANES:] = jnp.where(low, cos_r, cos)
    sin_ref[:, :LANES] = jnp.where(low, -sin, sin_r)
    sin_ref[:, LANES:] = jnp.where(low, -sin_r, sin)


def _rope_tables(positions):
    m = positions.size
    inv_a = ROPE_THETA ** (-jnp.arange(0, HEAD_DIM, 2, dtype=F32) / HEAD_DIM)
    inv_i = ROPE_THETA ** (-jnp.arange(0, IDX_DIM, 2, dtype=F32) / IDX_DIM)
    inv = jnp.concatenate([inv_a, inv_i, inv_i])[None, :]
    tm = 2048
    return pl.pallas_call(
        _rope_table_kernel,
        grid=(m // tm,),
        in_specs=[pl.BlockSpec((tm, 1), lambda i: (i, 0)),
                  pl.BlockSpec((1, LANES), lambda i: (0, 0))],
        out_specs=[pl.BlockSpec((tm, 2 * LANES), lambda i: (i, 0))] * 2,
        out_shape=[jax.ShapeDtypeStruct((m, 2 * LANES), F32)] * 2,
        compiler_params=_cparams(1),
        name="rope_tables",
    )(positions.reshape(m, 1), inv)


def _rmsnorm_kernel(x_ref, g_ref, o_ref):
    x = x_ref[...]
    y = x * lax.rsqrt(jnp.mean(x * x, axis=-1, keepdims=True) + NORM_EPS)
    o_ref[...] = (y * g_ref[...]).astype(o_ref.dtype)


def _rmsnorm(x2d, g, out_dtype):
    m, d = x2d.shape
    tm = 1024
    return pl.pallas_call(
        _rmsnorm_kernel,
        grid=(m // tm,),
        in_specs=[pl.BlockSpec((tm, d), lambda i: (i, 0)),
                  pl.BlockSpec((1, d), lambda i: (0, 0))],
        out_specs=pl.BlockSpec((tm, d), lambda i: (i, 0)),
        out_shape=jax.ShapeDtypeStruct((m, d), out_dtype),
        compiler_params=_cparams(1),
        name="rmsnorm",
    )(x2d, g.reshape(1, d))


ROPE_TN = 512


ROPE_TILES = 7


def _rope_epilogue(acc, jp, cos_ref, sin_ref, o_ref):
    cos_a, sin_a = cos_ref[:, :LANES], sin_ref[:, :LANES]
    cos_i, sin_i = cos_ref[:, LANES:], sin_ref[:, LANES:]
    q_scale = (HEAD_DIM ** -0.5) * LOG2E
    use_a = jnp.where(jp < 2, q_scale, jnp.where(jp < 4, 1.0, 0.0)).astype(F32)
    idx_tile = jnp.logical_and(jp >= 4, jp < 6)
    tail = jp == ROPE_TILES - 1
    for c in range(ROPE_TN // LANES):
        ik_lanes = c < 2
        use_i = jnp.where(jnp.logical_or(idx_tile, jnp.logical_and(tail, ik_lanes)),
                          1.0, 0.0).astype(F32)
        ident = jnp.where(jnp.logical_and(tail, not ik_lanes), 1.0, 0.0).astype(F32)
        cos = cos_a * use_a + cos_i * use_i + ident
        sin = sin_a * use_a + sin_i * use_i
        x = acc[:, c * LANES:(c + 1) * LANES]
        o_ref[:, c * LANES:(c + 1) * LANES] = (
            x * cos + pltpu.roll(x, LANES // 2, 1) * sin).astype(o_ref.dtype)


def _proj_rope_kernel(h_ref, w_ref, cos_ref, sin_ref, o_ref, acc0_ref, acc1_ref):
    step = pl.program_id(0)
    jp = jnp.maximum(step - 1, 0) % ROPE_TILES

    @pl.when(step == 0)
    def _():
        acc1_ref[...] = jnp.zeros(acc1_ref.shape, F32)

    def body(acc_new, acc_old):
        acc_new[...] = jnp.dot(h_ref[...], w_ref[...], preferred_element_type=F32)
        _rope_epilogue(acc_old[...], jp, cos_ref, sin_ref, o_ref)

    @pl.when(step % 2 == 0)
    def _():
        body(acc0_ref, acc1_ref)

    @pl.when(step % 2 == 1)
    def _():
        body(acc1_ref, acc0_ref)


def _proj_rope(h, w, cos, sin):
    m, d = h.shape
    n = w.shape[1]
    assert n == ROPE_TILES * ROPE_TN
    tm = 1024
    n_steps = (m // tm) * ROPE_TILES
    cur = lambda s: jnp.minimum(s, n_steps - 1)
    prev = lambda s: jnp.maximum(s - 1, 0)
    return pl.pallas_call(
        _proj_rope_kernel,
        grid=(n_steps + 1,),
        in_specs=[pl.BlockSpec((tm, d), lambda s: (cur(s) // ROPE_TILES, 0)),
                  pl.BlockSpec((d, ROPE_TN), lambda s: (0, cur(s) % ROPE_TILES)),
                  pl.BlockSpec((tm, 2 * LANES), lambda s: (prev(s) // ROPE_TILES, 0)),
                  pl.BlockSpec((tm, 2 * LANES), lambda s: (prev(s) // ROPE_TILES, 0))],
        out_specs=pl.BlockSpec(
            (tm, ROPE_TN), lambda s: (prev(s) // ROPE_TILES, prev(s) % ROPE_TILES)),
        out_shape=jax.ShapeDtypeStruct((m, n), BF16),
        scratch_shapes=[pltpu.VMEM((tm, ROPE_TN), F32)] * 2,
        compiler_params=_cparams(1),
        name="proj_rope",
    )(h, w, cos, sin)


def _proj_plain_kernel(h_ref, w_ref, o_ref):
    o_ref[...] = jnp.dot(h_ref[...], w_ref[...],
                         preferred_element_type=F32).astype(o_ref.dtype)


def _proj_plain(h, w, tn, name):
    m, d = h.shape
    n = w.shape[1]
    tm = 1024
    return pl.pallas_call(
        _proj_plain_kernel,
        grid=(m // tm, n // tn),
        in_specs=[pl.BlockSpec((tm, d), lambda i, j: (i, 0)),
                  pl.BlockSpec((d, tn), lambda i, j: (0, j))],
        out_specs=pl.BlockSpec((tm, tn), lambda i, j: (i, j)),
        out_shape=jax.ShapeDtypeStruct((m, n), BF16),
        compiler_params=_cparams(2),
        name=name,
    )(h, w)


ATT_TK = 256


VT_TILES = 2


def _proj_vt_kernel(h_ref, wt_ref, o_ref):
    vt = lax.dot_general(wt_ref[...], h_ref[...], NT_DIMS,
                         preferred_element_type=F32)
    row = lax.broadcasted_iota(jnp.int32, (vt.shape[0], 1), 0)
    vt = jnp.where(row % V_ROWS == HEAD_DIM, 1.0, vt).astype(o_ref.dtype)
    for j in range(VT_TILES):
        o_ref[j] = vt[:, j * ATT_TK:(j + 1) * ATT_TK]


def _proj_vt(h, wt, b, s):
    d = h.shape[1]
    n = wt.shape[0]
    tk = ATT_TK
    rows = VT_TILES * tk
    return pl.pallas_call(
        _proj_vt_kernel,
        grid=(b, s // rows),
        in_specs=[pl.BlockSpec((None, rows, d), lambda bi, t: (bi, t, 0)),
                  pl.BlockSpec((n, d), lambda bi, t: (0, 0))],
        out_specs=pl.BlockSpec((None, VT_TILES, n, tk), lambda bi, t: (bi, t, 0, 0)),
        out_shape=jax.ShapeDtypeStruct((b, s // tk, n, tk), BF16),
        compiler_params=_cparams(2),
        name="proj_vt",
    )(h.reshape(b, s, d), wt)


SCORE_TK = 512
COARSE_BITS = 16
BF16_STEP = 1 << 16
FINE_UNCHECKED = 10
N_CHAINS = 4
M_INIT = -1e30


def _key16_to_f32(u):
    key = u ^ 0x8000
    bits = jnp.where((key & 0x8000) != 0, key ^ 0x7FFF, key)
    return pltpu.bitcast(jnp.left_shift(bits, 16), F32)


def _f32_to_key32(x):
    bits = pltpu.bitcast(x, jnp.int32)
    return jnp.where(bits < 0, bits ^ 0x7FFFFFFF, bits)


def _key32_to_f32(key):
    return pltpu.bitcast(jnp.where(key < 0, key ^ 0x7FFFFFFF, key), F32)


def _dsa_kernel(q_ref, iq_ref, k_ref, ikw_ref, vt_ref, ga_ref, o_ref,
                sc_ref, sc16_ref, bias_ref, acc_ref, p_ref, *, topk):
    i = pl.program_id(1)
    tk = ATT_TK
    nk = ((i + 1) * Q_BLOCK + tk - 1) // tk
    q0 = pl.multiple_of(i * Q_BLOCK, Q_BLOCK)
    qidx = q0 + lax.broadcasted_iota(jnp.int32, (1, Q_BLOCK), 1)

    def tile_start(t):
        return pl.multiple_of(t * tk, tk)

    def key_index(k0):
        return k0 + lax.broadcasted_iota(jnp.int32, (tk, 1), 0)

    iq = iq_ref[...]
    n_pair = IDX_HEADS // 2
    lhs = jnp.concatenate(
        [iq[:, p * LANES:(p + 1) * LANES] for p in range(n_pair)], axis=0)
    w = ikw_ref[pl.ds(q0, Q_BLOCK), 2 * LANES:3 * LANES].astype(F32)
    wt = w.T * ((IDX_HEADS ** -0.5) * (IDX_DIM ** -0.5))
    wrow = [wt[h:h + 1, :] for h in range(IDX_HEADS)]

    def score_tile(t, carry):
        k0 = pl.multiple_of(t * SCORE_TK, SCORE_TK)
        acc = jnp.zeros((SCORE_TK, Q_BLOCK), F32)
        for e in range(2):
            ik_e = ikw_ref[pl.ds(k0, SCORE_TK), e * LANES:(e + 1) * LANES]
            d = lax.dot_general(ik_e, lhs, NT_DIMS,
                                preferred_element_type=F32)
            for p in range(n_pair):
                dp = d[:, p * Q_BLOCK:(p + 1) * Q_BLOCK]
                acc = acc + wrow[2 * p + e] * jnp.maximum(dp, 0.0)
        kidx = k0 + lax.broadcasted_iota(jnp.int32, (SCORE_TK, 1), 0)
        sc = jnp.where(kidx <= qidx, acc, NEG_INF)
        sc_ref[pl.ds(k0, SCORE_TK), :] = sc
        sc16_ref[pl.ds(k0, SCORE_TK), :] = sc.astype(BF16)
        return carry

    lax.fori_loop(0, ((i + 1) * Q_BLOCK + SCORE_TK - 1) // SCORE_TK, score_tile, 0)

    one16 = jnp.ones((), BF16)
    zero16 = jnp.zeros((), BF16)
    n_packed = tk // PACKED_SUBLANES

    def coarse(it, t_u):
        cand_u = t_u | jnp.left_shift(jnp.int32(1), COARSE_BITS - 1 - it)
        cand = _key16_to_f32(cand_u).astype(BF16)

        def count_tile(t, cnts):
            x = sc16_ref[pl.ds(tile_start(t), tk), :]
            ind = jnp.where(x >= cand, one16, zero16)
            cnts = list(cnts)
            for j in range(n_packed):
                cnts[j % N_CHAINS] = cnts[j % N_CHAINS] + ind[
                    j * PACKED_SUBLANES:(j + 1) * PACKED_SUBLANES, :]
            return tuple(cnts)

        cnts = lax.fori_loop(
            0, nk, count_tile,
            (jnp.zeros((PACKED_SUBLANES, Q_BLOCK), BF16),) * N_CHAINS)
        total = jnp.sum(sum(c.astype(F32) for c in cnts), axis=0, keepdims=True)
        return jnp.where(total >= topk, cand_u, t_u)

    t_u = lax.fori_loop(0, COARSE_BITS, coarse,
                        jnp.zeros((1, Q_BLOCK), jnp.int32))

    def count_ge(cand):
        def count_tile(t, cnts):
            s = sc_ref[pl.ds(tile_start(t), tk), :]
            ind = jnp.where(s >= cand, 1.0, 0.0)
            cnts = list(cnts)
            for j in range(tk // SUBLANES):
                cnts[j % N_CHAINS] = cnts[j % N_CHAINS] + ind[
                    j * SUBLANES:(j + 1) * SUBLANES, :]
            return tuple(cnts)

        cnts = lax.fori_loop(
            0, nk, count_tile,
            (jnp.zeros((SUBLANES, Q_BLOCK), F32),) * N_CHAINS)
        return jnp.sum(sum(cnts), axis=0, keepdims=True)

    base = _f32_to_key32(_key16_to_f32(t_u))
    lo0 = base - BF16_STEP // 2
    hi0 = base + BF16_STEP + 1

    def fine_cond(st):
        lo, hi, cnt_lo = st
        open_ = jnp.logical_and(cnt_lo != topk, hi - lo > 1)
        return jnp.max(jnp.where(open_, 1.0, 0.0)) > 0.0

    def fine_step(st):
        lo, hi, cnt_lo = st
        mid = lo + jnp.right_shift(hi - lo, 1)
        cnt = count_ge(_key32_to_f32(mid))
        ge = cnt >= topk
        return (jnp.where(ge, mid, lo), jnp.where(ge, hi, mid),
                jnp.where(ge, cnt, cnt_lo))

    st = (lo0, hi0, jnp.full((1, Q_BLOCK), -1.0, F32))
    st = lax.fori_loop(0, FINE_UNCHECKED, lambda _, s_: fine_step(s_), st)
    lo, _, cnt_lo = lax.while_loop(
        fine_cond, lambda s_: fine_step(fine_step(s_)), st)
    full_row = qidx < topk
    thr = jnp.where(full_row, NEG_INF, _key32_to_f32(lo))

    def causal_bias(k0):
        return jnp.where(key_index(k0) <= qidx, 0.0, NEG_INF)

    def write_bias():
        def bias_tile(t, carry):
            k0 = tile_start(t)
            s = sc_ref[pl.ds(k0, tk), :]
            bias_ref[pl.ds(k0, tk), :] = jnp.where(s >= thr, causal_bias(k0), NEG_INF)
            return carry

        lax.fori_loop(0, nk, bias_tile, 0)

    def write_bias_with_ties():
        def count_where(pred):
            def count_tile(t, cnt):
                k0 = tile_start(t)
                ind = jnp.where(pred(sc_ref[pl.ds(k0, tk), :], key_index(k0)), 1.0, 0.0)
                return cnt + jnp.sum(ind, axis=0, keepdims=True)

            return lax.fori_loop(0, nk, count_tile, jnp.zeros((1, Q_BLOCK), F32))

        n_tied = topk - count_where(lambda s, kidx: s > thr)

        def index_step(_, st_):
            lo_i, hi_i = st_
            mid = lo_i + jnp.right_shift(hi_i - lo_i, 1)
            cnt = count_where(
                lambda s, kidx: jnp.logical_and(s == thr, kidx <= mid))
            ge = cnt >= n_tied
            return jnp.where(ge, lo_i, mid), jnp.where(ge, mid, hi_i)

        n_keys = nk * tk
        steps = int(np.ceil(np.log2(sc_ref.shape[0]))) + 1
        _, last = lax.fori_loop(
            0, steps, index_step,
            (jnp.full((1, Q_BLOCK), -1, jnp.int32),
             jnp.full((1, Q_BLOCK), 1, jnp.int32) * (n_keys - 1)))

        def bias_tile(t, carry):
            k0 = tile_start(t)
            s = sc_ref[pl.ds(k0, tk), :]
            keep = jnp.logical_or(
                s > thr, jnp.logical_and(s == thr, key_index(k0) <= last))
            bias_ref[pl.ds(k0, tk), :] = jnp.where(keep, causal_bias(k0), NEG_INF)
            return carry

        lax.fori_loop(0, nk, bias_tile, 0)

    tied = jnp.logical_and(cnt_lo != topk, jnp.logical_not(full_row))
    lax.cond(jnp.max(jnp.where(tied, 1.0, 0.0)) > 0.0,
             write_bias_with_ties, write_bias)

    acc_ref[...] = jnp.zeros(acc_ref.shape, F32)

    def qk_head(t, h, m_old):
        k0 = tile_start(t)
        hs = slice(h * HEAD_DIM, (h + 1) * HEAD_DIM)
        s = lax.dot_general(k_ref[pl.ds(k0, tk), hs], q_ref[:, hs], NT_DIMS,
                            preferred_element_type=F32)
        s = s + bias_ref[pl.ds(k0, tk), :]
        m_new = jnp.maximum(m_old, jnp.max(s, axis=0, keepdims=True))
        p_ref[t % 2, h] = jnp.exp2(s - m_new).astype(BF16)
        return m_new, jnp.exp2(m_old - m_new)

    def pv_head(t, h, alpha):
        pv = jnp.dot(vt_ref[t, h * V_ROWS:(h + 1) * V_ROWS, :], p_ref[t % 2, h],
                     preferred_element_type=F32)
        acc_ref[h] = acc_ref[h] * alpha + pv

    def qk_stage(t, ms):
        pairs = [qk_head(t, h, ms[h]) for h in range(N_HEADS)]
        return tuple(p[0] for p in pairs), tuple(p[1] for p in pairs)

    def pv_stage(t, alphas):
        for h in range(N_HEADS):
            pv_head(t, h, alphas[h])

    def att_tile(t, carry):
        ms, alphas = carry
        pairs = []
        for h in range(N_HEADS):
            pv_head(t - 1, h, alphas[h])
            pairs.append(qk_head(t, h, ms[h]))
        return tuple(p[0] for p in pairs), tuple(p[1] for p in pairs)

    m0 = tuple(jnp.full((1, Q_BLOCK), M_INIT, F32) for _ in range(N_HEADS))
    carry = lax.fori_loop(
        0, (nk - 1) // 2,
        lambda j, c: att_tile(2 * j + 2, att_tile(2 * j + 1, c)),
        qk_stage(0, m0))
    carry = lax.cond((nk - 1) % 2 == 1,
                     lambda c: att_tile(nk - 1, c), lambda c: c, carry)
    pv_stage(nk - 1, carry[1])

    for h in range(N_HEADS):
        hs = slice(h * HEAD_DIM, (h + 1) * HEAD_DIM)
        out_t = acc_ref[h, :HEAD_DIM, :] * (1.0 / acc_ref[h, HEAD_DIM:HEAD_DIM + 1, :])
        gate = ga_ref[:, hs].astype(F32)
        o_ref[:, hs] = (out_t.T * (gate * _sigmoid(gate))).astype(o_ref.dtype)


def _dsa_attention(pr, pv, vt, b, s):
    pr3 = pr.reshape(b, s, pr.shape[1])
    pv3 = pv.reshape(b, s, pv.shape[1])
    topk = min(TOPK_MAX, s // 4)
    w_att = ATTN_WIDTH
    single = pl.Buffered(1)
    return pl.pallas_call(
        functools.partial(_dsa_kernel, topk=topk),
        grid=(b, s // Q_BLOCK),
        in_specs=[
            pl.BlockSpec((None, Q_BLOCK, w_att), lambda bi, i: (bi, i, 0)),
            pl.BlockSpec((None, Q_BLOCK, w_att), lambda bi, i: (bi, i, 2)),
            pl.BlockSpec((None, s, w_att), lambda bi, i: (bi, 0, 1),
                         pipeline_mode=single),
            pl.BlockSpec((None, s, 3 * LANES), lambda bi, i: (bi, 0, 8),
                         pipeline_mode=single),
            pl.BlockSpec((None, s // ATT_TK, N_HEADS * V_ROWS, ATT_TK),
                         lambda bi, i: (bi, 0, 0, 0), pipeline_mode=single),
            pl.BlockSpec((None, Q_BLOCK, w_att), lambda bi, i: (bi, i, 0)),
        ],
        out_specs=pl.BlockSpec((None, Q_BLOCK, w_att), lambda bi, i: (bi, i, 0)),
        out_shape=jax.ShapeDtypeStruct((b, s, w_att), BF16),
        scratch_shapes=[pltpu.VMEM((s, Q_BLOCK), F32),
                        pltpu.VMEM((s, Q_BLOCK), BF16),
                        pltpu.VMEM((s, Q_BLOCK), F32),
                        pltpu.VMEM((N_HEADS, V_ROWS, Q_BLOCK), F32),
                        pltpu.VMEM((2, N_HEADS, ATT_TK, Q_BLOCK), BF16)],
        compiler_params=_cparams(2),
        name="dsa_attention",
    )(pr3, pr3, pr3, pr3, vt, pv3)


RNN_T = 512
N_RNN_TILES = D_RNN // LANES
GATE_WIN = 3


def _gate_window_start(j):
    assert RNN_BLOCK_DIM <= LANES
    return min(max(j - 1, 0), N_RNN_TILES - GATE_WIN)


def _rglru_kernel(xr_ref, gr_ref, cw_ref, cb_ref, wg_ref, bg_ref, lam_ref,
                  o_ref, xp_ref, a_ref, u_ref, h_ref, hc_ref):
    c = pl.program_id(1)
    t_len = RNN_T
    pad = SUBLANES

    @pl.when(c == 0)
    def _():
        xp_ref[0:pad, :] = jnp.zeros((pad, D_RNN), F32)
        hc_ref[...] = jnp.zeros((1, D_RNN), F32)

    x = xr_ref[...].astype(F32)
    xp_ref[pad:pad + t_len, :] = x
    y = cb_ref[...] + cw_ref[CONV_WIDTH - 1:CONV_WIDTH, :] * x
    for j in range(CONV_WIDTH - 1):
        back = CONV_WIDTH - 1 - j
        y = y + cw_ref[j:j + 1, :] * xp_ref[pad - back:pad - back + t_len, :]
    xp_ref[0:pad, :] = x[t_len - pad:, :]

    y16 = y.astype(BF16)
    yh = 0.5 * y
    z = -lam_ref[...]
    softplus = jnp.maximum(z, 0.0) + jnp.log1p(jnp.exp(-jnp.abs(z)))
    k = (-0.5 * LRU_C * LOG2E) * softplus
    for j in range(N_RNN_TILES):
        ls = slice(j * LANES, (j + 1) * LANES)
        w0 = _gate_window_start(j) * LANES
        g = jnp.dot(y16[:, w0:w0 + GATE_WIN * LANES], wg_ref[j],
                    preferred_element_type=F32) + bg_ref[j]
        t_r = jnp.tanh(g[:, :LANES])
        t_i = jnp.tanh(g[:, LANES:])
        a = jnp.exp2(k[:, ls] * t_r + k[:, ls])
        a_ref[:, ls] = a
        om = 1.0 - a * a
        root = jnp.where(om > 0.0, om * lax.rsqrt(om), 0.0)
        u_ref[:, ls] = root * (t_i * yh[:, ls] + yh[:, ls])

    def step(t, h):
        h = a_ref[pl.ds(t, 1), :] * h + u_ref[pl.ds(t, 1), :]
        h_ref[pl.ds(t, 1), :] = h
        return h

    hc_ref[...] = lax.fori_loop(0, t_len, step, hc_ref[...], unroll=8)
    gh = 0.5 * gr_ref[...].astype(F32)
    o_ref[...] = (h_ref[...] * (gh * jnp.tanh(gh) + gh)).astype(o_ref.dtype)


def _rglru(pc, b, s, conv_w, conv_b, w_gates, b_gates, lam):
    pc3 = pc.reshape(b, s, pc.shape[1])
    t = RNN_T
    const = lambda bi, c: (0, 0)
    const3 = lambda bi, c: (0, 0, 0)
    return pl.pallas_call(
        _rglru_kernel,
        grid=(b, s // t),
        in_specs=[
            pl.BlockSpec((None, t, D_RNN), lambda bi, c: (bi, c, 0)),
            pl.BlockSpec((None, t, D_RNN), lambda bi, c: (bi, c, 1)),
            pl.BlockSpec((CONV_WIDTH, D_RNN), const),
            pl.BlockSpec((1, D_RNN), const),
            pl.BlockSpec((N_RNN_TILES, GATE_WIN * LANES, 2 * LANES), const3),
            pl.BlockSpec((N_RNN_TILES, 1, 2 * LANES), const3),
            pl.BlockSpec((1, D_RNN), const),
        ],
        out_specs=pl.BlockSpec((None, t, D_RNN), lambda bi, c: (bi, c, 0)),
        out_shape=jax.ShapeDtypeStruct((b, s, D_RNN), BF16),
        scratch_shapes=[pltpu.VMEM((t + SUBLANES, D_RNN), F32),
                        pltpu.VMEM((t, D_RNN), F32),
                        pltpu.VMEM((t, D_RNN), F32),
                        pltpu.VMEM((t, D_RNN), F32),
                        pltpu.VMEM((1, D_RNN), F32)],
        compiler_params=_cparams(2),
        name="rglru",
    )(pc3, pc3, conv_w, conv_b.reshape(1, D_RNN), w_gates, b_gates,
      lam.reshape(1, D_RNN))


def _pack_gates(w_rg_l, b_rg_l, w_ig_l, b_ig_l):
    dense = [jax.scipy.linalg.block_diag(*w) for w in (w_rg_l, w_ig_l)]
    bands, biases = [], []
    for j in range(N_RNN_TILES):
        w0 = _gate_window_start(j) * LANES
        cols = slice(j * LANES, (j + 1) * LANES)
        bands.append(jnp.concatenate(
            [dm[w0:w0 + GATE_WIN * LANES, cols] for dm in dense], axis=1))
        biases.append(jnp.concatenate([b_rg_l[cols], b_ig_l[cols]])[None, :])
    return (0.5 * jnp.stack(bands)).astype(BF16), 0.5 * jnp.stack(biases)


def _merge_kernel(at_ref, hr_ref, ma_ref, mb_ref, x_ref, woa_ref, wor_ref,
                  wo_ref, g_ref, *out_refs):
    y_a = jnp.dot(at_ref[...], woa_ref[...], preferred_element_type=F32)
    y_b = jnp.dot(hr_ref[...], wor_ref[...], preferred_element_type=F32)
    merged = (_sigmoid(ma_ref[...].astype(F32)) * y_a
              + _sigmoid(mb_ref[...].astype(F32)) * y_b)
    x = x_ref[...] + jnp.dot(merged.astype(BF16), wo_ref[...],
                             preferred_element_type=F32)
    no_ref = out_refs[-1]
    if len(out_refs) == 2:
        out_refs[0][...] = x
    y = x * lax.rsqrt(jnp.mean(x * x, axis=-1, keepdims=True) + NORM_EPS)
    no_ref[...] = (y * g_ref[...]).astype(no_ref.dtype)


def _merge(attn_g, hr_g, pv, x2d, w_oa, w_or, w_o, g, last):
    m, d = x2d.shape
    tm = 512
    const = lambda i: (0, 0)
    row = pl.BlockSpec((tm, d), lambda i: (i, 0))
    normed = jax.ShapeDtypeStruct((m, d), F32 if last else BF16)
    outs = pl.pallas_call(
        _merge_kernel,
        grid=(m // tm,),
        in_specs=[
            pl.BlockSpec((tm, ATTN_WIDTH), lambda i: (i, 0)),
            pl.BlockSpec((tm, D_RNN), lambda i: (i, 0)),
            pl.BlockSpec((tm, d), lambda i: (i, 1)),
            pl.BlockSpec((tm, d), lambda i: (i, 2)),
            row,
            pl.BlockSpec((ATTN_WIDTH, d), const),
            pl.BlockSpec((D_RNN, d), const),
            pl.BlockSpec((d, d), const),
            pl.BlockSpec((1, d), const),
        ],
        out_specs=[row] if last else [row, row],
        out_shape=[normed] if last else [jax.ShapeDtypeStruct((m, d), F32), normed],
        compiler_params=_cparams(1),
        name="merge",
    )(attn_g, hr_g, pv, pv, x2d, w_oa, w_or, w_o, g.reshape(1, d))
    return (None, outs[0]) if last else tuple(outs)


def _pack_weights(w_in):
    sizes = (ATTN_WIDTH,) * 4 + (IDX_HEADS * IDX_DIM, IDX_DIM, IDX_HEADS,
                                 D_RNN, D_RNN, D_MODEL, D_MODEL)
    (o_q, o_k, o_v, o_ga, o_iq, o_ik, o_iw, o_xr, o_gr, o_ma,
     o_mb, o_end) = np.concatenate([[0], np.cumsum(sizes)]).tolist()
    w16 = w_in.astype(BF16)
    n_l, d = w_in.shape[:2]
    half = IDX_DIM // 2
    z = lambda n: jnp.zeros((n_l, d, n), BF16)
    iq = w16[:, :, o_iq:o_ik].reshape(n_l, d, IDX_HEADS // 2, 2, 2, half).transpose(
        0, 1, 2, 4, 3, 5).reshape(n_l, d, IDX_HEADS * IDX_DIM)
    ik1, ik2 = w16[:, :, o_ik:o_ik + half], w16[:, :, o_ik + half:o_iw]
    w_rope = jnp.concatenate(
        [w16[:, :, o_q:o_v], iq,
         ik1, z(half), ik2, z(half), z(half), ik1, z(half), ik2,
         w16[:, :, o_iw:o_xr], z(ROPE_TN - 2 * LANES - IDX_HEADS)], axis=2)
    w_gate = jnp.concatenate([w16[:, :, o_ga:o_iq], w16[:, :, o_ma:o_end]], axis=2)
    vt = jnp.swapaxes(w16[:, :, o_v:o_ga], 1, 2).reshape(n_l, N_HEADS, HEAD_DIM, d)
    w_vt = jnp.pad(vt, ((0, 0), (0, 0), (0, V_ROWS - HEAD_DIM), (0, 0))).reshape(
        n_l, N_HEADS * V_ROWS, d)
    w_rnn = w16[:, :, o_xr:o_ma]
    return w_rope, w_gate, w_vt, w_rnn


def kernel(x, positions, norm_g, w_in, conv_w, conv_b, w_rg, b_rg, w_ig, b_ig,
           lru_lambda, w_out_attn, w_out_rnn, w_o, final_g):
    b, s, d = x.shape
    depth = w_in.shape[0]
    m = b * s
    cos, sin = _rope_tables(positions)
    x2d = x.reshape(m, d)
    h = _rmsnorm(x2d, norm_g[0], BF16)
    out = None
    w_rope, w_gate, w_vt, w_rnn = _pack_weights(w_in)
    w_oa, w_or, w_oo = (w.astype(BF16) for w in (w_out_attn, w_out_rnn, w_o))
    for l in range(depth):
        w_gates, b_gates = _pack_gates(w_rg[l], b_rg[l], w_ig[l], b_ig[l])
        pr = _proj_rope(h, w_rope[l], cos, sin)
        pv = _proj_plain(h, w_gate[l], 1024, "proj_gate")
        vt = _proj_vt(h, w_vt[l], b, s)
        pc = _proj_plain(h, w_rnn[l], D_RNN, "proj_rnn")
        attn_g = _dsa_attention(pr, pv, vt, b, s).reshape(m, ATTN_WIDTH)
        hr_g = _rglru(pc, b, s, conv_w[l], conv_b[l], w_gates, b_gates,
                      lru_lambda[l]).reshape(m, D_RNN)
        last = l == depth - 1
        g_next = final_g if last else norm_g[l + 1]
        x2d, normed = _merge(attn_g, hr_g, pv, x2d, w_oa[l], w_or[l], w_oo[l],
                             g_next, last)
        if last:
            out = normed
        else:
            h = normed
    return out.reshape(b, s, d)
```

```python
import functools

import jax
import jax.numpy as jnp
import numpy as np
from jax import lax
from jax.experimental import pallas as pl
from jax.experimental.pallas import tpu as pltpu

D_MODEL = 1024
N_HEADS = 8
HEAD_DIM = 128
ATTN_WIDTH = N_HEADS * HEAD_DIM
IDX_HEADS = 16
IDX_DIM = 64
TOPK_MAX = 256
Q_BLOCK = 256
D_RNN = 1408
RNN_BLOCKS = 16
RNN_BLOCK_DIM = D_RNN // RNN_BLOCKS
CONV_WIDTH = 4
LRU_C = 8.0
ROPE_THETA = 10000.0
NORM_EPS = 1e-6

LANES = 128
SUBLANES = 8
PACKED_SUBLANES = 16
VMEM_LIMIT = 56 * 1024 * 1024

F32 = jnp.float32
BF16 = jnp.bfloat16
NEG_INF = float("-inf")
NT_DIMS = (((1,), (1,)), ((), ()))
LOG2E = float(np.log2(np.e))
V_ROWS = HEAD_DIM + SUBLANES


def _sigmoid(x):
    return 0.5 * jnp.tanh(0.5 * x) + 0.5


def _cparams(n_grid):
    return pltpu.CompilerParams(
        dimension_semantics=("arbitrary",) * n_grid,
        vmem_limit_bytes=VMEM_LIMIT)


def _rope_table_kernel(pos_ref, inv_ref, cos_ref, sin_ref):
    ang = pos_ref[...].astype(F32) * inv_ref[...]
    cos, sin = jnp.cos(ang), jnp.sin(ang)
    cos_r, sin_r = pltpu.roll(cos, LANES // 2, 1), pltpu.roll(sin, LANES // 2, 1)
    low = lax.broadcasted_iota(jnp.int32, (1, LANES), 1) < LANES // 2
    cos_ref[:, :LANES] = jnp.where(low, cos, cos_r)
    cos_ref[:, LANES:] = jnp.where(low, cos_r, cos)
    sin_ref[:, :LANES] = jnp.where(low, -sin, sin_r)
    sin_ref[:, LANES:] = jnp.where(low, -sin_r, sin)


def _rope_tables(positions):
    m = positions.size
    inv_a = ROPE_THETA ** (-jnp.arange(0, HEAD_DIM, 2, dtype=F32) / HEAD_DIM)
    inv_i = ROPE_THETA ** (-jnp.arange(0, IDX_DIM, 2, dtype=F32) / IDX_DIM)
    inv = jnp.concatenate([inv_a, inv_i, inv_i])[None, :]
    tm = 2048
    return pl.pallas_call(
        _rope_table_kernel,
        grid=(m // tm,),
        in_specs=[pl.BlockSpec((tm, 1), lambda i: (i, 0)),
                  pl.BlockSpec((1, LANES), lambda i: (0, 0))],
        out_specs=[pl.BlockSpec((tm, 2 * LANES), lambda i: (i, 0))] * 2,
        out_shape=[jax.ShapeDtypeStruct((m, 2 * LANES), F32)] * 2,
        compiler_params=_cparams(1),
        name="rope_tables",
    )(positions.reshape(m, 1), inv)


def _rmsnorm_kernel(x_ref, g_ref, o_ref):
    x = x_ref[...]
    y = x * lax.rsqrt(jnp.mean(x * x, axis=-1, keepdims=True) + NORM_EPS)
    o_ref[...] = (y * g_ref[...]).astype(o_ref.dtype)


def _rmsnorm(x2d, g, out_dtype):
    m, d = x2d.shape
    tm = 1024
    return pl.pallas_call(
        _rmsnorm_kernel,
        grid=(m // tm,),
        in_specs=[pl.BlockSpec((tm, d), lambda i: (i, 0)),
                  pl.BlockSpec((1, d), lambda i: (0, 0))],
        out_specs=pl.BlockSpec((tm, d), lambda i: (i, 0)),
        out_shape=jax.ShapeDtypeStruct((m, d), out_dtype),
        compiler_params=_cparams(1),
        name="rmsnorm",
    )(x2d, g.reshape(1, d))


ROPE_TN = 512


ROPE_TILES = 7


def _rope_epilogue(acc, jp, cos_ref, sin_ref, o_ref):
    cos_a, sin_a = cos_ref[:, :LANES], sin_ref[:, :LANES]
    cos_i, sin_i = cos_ref[:, LANES:], sin_ref[:, LANES:]
    q_scale = (HEAD_DIM ** -0.5) * LOG2E
    use_a = jnp.where(jp < 2, q_scale, jnp.where(jp < 4, 1.0, 0.0)).astype(F32)
    idx_tile = jnp.logical_and(jp >= 4, jp < 6)
    tail = jp == ROPE_TILES - 1
    for c in range(ROPE_TN // LANES):
        ik_lanes = c < 2
        use_i = jnp.where(jnp.logical_or(idx_tile, jnp.logical_and(tail, ik_lanes)),
                          1.0, 0.0).astype(F32)
        ident = jnp.where(jnp.logical_and(tail, not ik_lanes), 1.0, 0.0).astype(F32)
        cos = cos_a * use_a + cos_i * use_i + ident
        sin = sin_a * use_a + sin_i * use_i
        x = acc[:, c * LANES:(c + 1) * LANES]
        o_ref[:, c * LANES:(c + 1) * LANES] = (
            x * cos + pltpu.roll(x, LANES // 2, 1) * sin).astype(o_ref.dtype)


def _proj_rope_kernel(h_ref, w_ref, cos_ref, sin_ref, o_ref, acc0_ref, acc1_ref):
    step = pl.program_id(0)
    jp = jnp.maximum(step - 1, 0) % ROPE_TILES

    @pl.when(step == 0)
    def _():
        acc1_ref[...] = jnp.zeros(acc1_ref.shape, F32)

    def body(acc_new, acc_old):
        acc_new[...] = jnp.dot(h_ref[...], w_ref[...], preferred_element_type=F32)
        _rope_epilogue(acc_old[...], jp, cos_ref, sin_ref, o_ref)

    @pl.when(step % 2 == 0)
    def _():
        body(acc0_ref, acc1_ref)

    @pl.when(step % 2 == 1)
    def _():
        body(acc1_ref, acc0_ref)


def _proj_rope(h, w, cos, sin):
    m, d = h.shape
    n = w.shape[1]
    assert n == ROPE_TILES * ROPE_TN
    tm = 1024
    n_steps = (m // tm) * ROPE_TILES
    cur = lambda s: jnp.minimum(s, n_steps - 1)
    prev = lambda s: jnp.maximum(s - 1, 0)
    return pl.pallas_call(
        _proj_rope_kernel,
        grid=(n_steps + 1,),
        in_specs=[pl.BlockSpec((tm, d), lambda s: (cur(s) // ROPE_TILES, 0)),
                  pl.BlockSpec((d, ROPE_TN), lambda s: (0, cur(s) % ROPE_TILES)),
                  pl.BlockSpec((tm, 2 * LANES), lambda s: (prev(s) // ROPE_TILES, 0)),
                  pl.BlockSpec((tm, 2 * LANES), lambda s: (prev(s) // ROPE_TILES, 0))],
        out_specs=pl.BlockSpec(
            (tm, ROPE_TN), lambda s: (prev(s) // ROPE_TILES, prev(s) % ROPE_TILES)),
        out_shape=jax.ShapeDtypeStruct((m, n), BF16),
        scratch_shapes=[pltpu.VMEM((tm, ROPE_TN), F32)] * 2,
        compiler_params=_cparams(1),
        name="proj_rope",
    )(h, w, cos, sin)


def _proj_plain_kernel(h_ref, w_ref, o_ref):
    o_ref[...] = jnp.dot(h_ref[...], w_ref[...],
                         preferred_element_type=F32).astype(o_ref.dtype)


def _proj_plain(h, w, tn, name):
    m, d = h.shape
    n = w.shape[1]
    tm = 2048
    return pl.pallas_call(
        _proj_plain_kernel,
        grid=(m // tm, n // tn),
        in_specs=[pl.BlockSpec((tm, d), lambda i, j: (i, 0)),
                  pl.BlockSpec((d, tn), lambda i, j: (0, j))],
        out_specs=pl.BlockSpec((tm, tn), lambda i, j: (i, j)),
        out_shape=jax.ShapeDtypeStruct((m, n), BF16),
        compiler_params=_cparams(2),
        name=name,
    )(h, w)


ATT_TK = 256


VT_TILES = 2


def _proj_vt_kernel(h_ref, wt_ref, o_ref):
    vt = lax.dot_general(wt_ref[...], h_ref[...], NT_DIMS,
                         preferred_element_type=F32)
    row = lax.broadcasted_iota(jnp.int32, (vt.shape[0], 1), 0)
    vt = jnp.where(row % V_ROWS == HEAD_DIM, 1.0, vt).astype(o_ref.dtype)
    for j in range(VT_TILES):
        o_ref[j] = vt[:, j * ATT_TK:(j + 1) * ATT_TK]


def _proj_vt(h, wt, b, s):
    d = h.shape[1]
    n = wt.shape[0]
    tk = ATT_TK
    rows = VT_TILES * tk
    return pl.pallas_call(
        _proj_vt_kernel,
        grid=(b, s // rows),
        in_specs=[pl.BlockSpec((None, rows, d), lambda bi, t: (bi, t, 0)),
                  pl.BlockSpec((n, d), lambda bi, t: (0, 0))],
        out_specs=pl.BlockSpec((None, VT_TILES, n, tk), lambda bi, t: (bi, t, 0, 0)),
        out_shape=jax.ShapeDtypeStruct((b, s // tk, n, tk), BF16),
        compiler_params=_cparams(2),
        name="proj_vt",
    )(h.reshape(b, s, d), wt)


SCORE_TK = 512
COARSE_BITS = 16
BF16_STEP = 1 << 16
FINE_UNCHECKED = 10
N_CHAINS = 4
M_INIT = -1e30


def _key16_to_f32(u):
    key = u ^ 0x8000
    bits = jnp.where((key & 0x8000) != 0, key ^ 0x7FFF, key)
    return pltpu.bitcast(jnp.left_shift(bits, 16), F32)


def _f32_to_key32(x):
    bits = pltpu.bitcast(x, jnp.int32)
    return jnp.where(bits < 0, bits ^ 0x7FFFFFFF, bits)


def _key32_to_f32(key):
    return pltpu.bitcast(jnp.where(key < 0, key ^ 0x7FFFFFFF, key), F32)


def _dsa_kernel(q_ref, iq_ref, k_ref, ikw_ref, vt_ref, ga_ref, o_ref,
                sc_ref, sc16_ref, bias_ref, acc_ref, p_ref, *, topk):
    i = pl.program_id(1)
    tk = ATT_TK
    nk = ((i + 1) * Q_BLOCK + tk - 1) // tk
    q0 = pl.multiple_of(i * Q_BLOCK, Q_BLOCK)
    qidx = q0 + lax.broadcasted_iota(jnp.int32, (1, Q_BLOCK), 1)

    def tile_start(t):
        return pl.multiple_of(t * tk, tk)

    def key_index(k0):
        return k0 + lax.broadcasted_iota(jnp.int32, (tk, 1), 0)

    iq = iq_ref[...]
    n_pair = IDX_HEADS // 2
    lhs = jnp.concatenate(
        [iq[:, p * LANES:(p + 1) * LANES] for p in range(n_pair)], axis=0)
    w = ikw_ref[pl.ds(q0, Q_BLOCK), 2 * LANES:3 * LANES].astype(F32)
    wt = w.T * ((IDX_HEADS ** -0.5) * (IDX_DIM ** -0.5))
    wrow = [wt[h:h + 1, :] for h in range(IDX_HEADS)]

    def score_tile(t, carry):
        k0 = pl.multiple_of(t * SCORE_TK, SCORE_TK)
        acc = jnp.zeros((SCORE_TK, Q_BLOCK), F32)
        for e in range(2):
            ik_e = ikw_ref[pl.ds(k0, SCORE_TK), e * LANES:(e + 1) * LANES]
            d = lax.dot_general(ik_e, lhs, NT_DIMS,
                                preferred_element_type=F32)
            for p in range(n_pair):
                dp = d[:, p * Q_BLOCK:(p + 1) * Q_BLOCK]
                acc = acc + wrow[2 * p + e] * jnp.maximum(dp, 0.0)
        kidx = k0 + lax.broadcasted_iota(jnp.int32, (SCORE_TK, 1), 0)
        sc = jnp.where(kidx <= qidx, acc, NEG_INF)
        sc_ref[pl.ds(k0, SCORE_TK), :] = sc
        sc16_ref[pl.ds(k0, SCORE_TK), :] = sc.astype(BF16)
        return carry

    lax.fori_loop(0, ((i + 1) * Q_BLOCK + SCORE_TK - 1) // SCORE_TK, score_tile, 0)

    one16 = jnp.ones((), BF16)
    zero16 = jnp.zeros((), BF16)
    n_packed = tk // PACKED_SUBLANES

    def coarse(it, t_u):
        cand_u = t_u | jnp.left_shift(jnp.int32(1), COARSE_BITS - 1 - it)
        cand = _key16_to_f32(cand_u).astype(BF16)

        def count_tile(t, cnts):
            x = sc16_ref[pl.ds(tile_start(t), tk), :]
            ind = jnp.where(x >= cand, one16, zero16)
            cnts = list(cnts)
            for j in range(n_packed):
                cnts[j % N_CHAINS] = cnts[j % N_CHAINS] + ind[
                    j * PACKED_SUBLANES:(j + 1) * PACKED_SUBLANES, :]
            return tuple(cnts)

        cnts = lax.fori_loop(
            0, nk, count_tile,
            (jnp.zeros((PACKED_SUBLANES, Q_BLOCK), BF16),) * N_CHAINS)
        total = jnp.sum(sum(c.astype(F32) for c in cnts), axis=0, keepdims=True)
        return jnp.where(total >= topk, cand_u, t_u)

    t_u = lax.fori_loop(0, COARSE_BITS, coarse,
                        jnp.zeros((1, Q_BLOCK), jnp.int32))

    def count_ge(cand):
        def count_tile(t, cnts):
            s = sc_ref[pl.ds(tile_start(t), tk), :]
            ind = jnp.where(s >= cand, 1.0, 0.0)
            cnts = list(cnts)
            for j in range(tk // SUBLANES):
                cnts[j % N_CHAINS] = cnts[j % N_CHAINS] + ind[
                    j * SUBLANES:(j + 1) * SUBLANES, :]
            return tuple(cnts)

        cnts = lax.fori_loop(
            0, nk, count_tile,
            (jnp.zeros((SUBLANES, Q_BLOCK), F32),) * N_CHAINS)
        return jnp.sum(sum(cnts), axis=0, keepdims=True)

    base = _f32_to_key32(_key16_to_f32(t_u))
    lo0 = base - BF16_STEP // 2
    hi0 = base + BF16_STEP + 1

    def fine_cond(st):
        lo, hi, cnt_lo = st
        open_ = jnp.logical_and(cnt_lo != topk, hi - lo > 1)
        return jnp.max(jnp.where(open_, 1.0, 0.0)) > 0.0

    def fine_step(st):
        lo, hi, cnt_lo = st
        mid = lo + jnp.right_shift(hi - lo, 1)
        cnt = count_ge(_key32_to_f32(mid))
        ge = cnt >= topk
        return (jnp.where(ge, mid, lo), jnp.where(ge, hi, mid),
                jnp.where(ge, cnt, cnt_lo))

    st = (lo0, hi0, jnp.full((1, Q_BLOCK), -1.0, F32))
    st = lax.fori_loop(0, FINE_UNCHECKED, lambda _, s_: fine_step(s_), st)
    lo, _, cnt_lo = lax.while_loop(
        fine_cond, lambda s_: fine_step(fine_step(s_)), st)
    full_row = qidx < topk
    thr = jnp.where(full_row, NEG_INF, _key32_to_f32(lo))

    def causal_bias(k0):
        return jnp.where(key_index(k0) <= qidx, 0.0, NEG_INF)

    def write_bias():
        def bias_tile(t, carry):
            k0 = tile_start(t)
            s = sc_ref[pl.ds(k0, tk), :]
            bias_ref[pl.ds(k0, tk), :] = jnp.where(s >= thr, causal_bias(k0), NEG_INF)
            return carry

        lax.fori_loop(0, nk, bias_tile, 0)

    def write_bias_with_ties():
        def count_where(pred):
            def count_tile(t, cnt):
                k0 = tile_start(t)
                ind = jnp.where(pred(sc_ref[pl.ds(k0, tk), :], key_index(k0)), 1.0, 0.0)
                return cnt + jnp.sum(ind, axis=0, keepdims=True)

            return lax.fori_loop(0, nk, count_tile, jnp.zeros((1, Q_BLOCK), F32))

        n_tied = topk - count_where(lambda s, kidx: s > thr)

        def index_step(_, st_):
            lo_i, hi_i = st_
            mid = lo_i + jnp.right_shift(hi_i - lo_i, 1)
            cnt = count_where(
                lambda s, kidx: jnp.logical_and(s == thr, kidx <= mid))
            ge = cnt >= n_tied
            return jnp.where(ge, lo_i, mid), jnp.where(ge, mid, hi_i)

        n_keys = nk * tk
        steps = int(np.ceil(np.log2(sc_ref.shape[0]))) + 1
        _, last = lax.fori_loop(
            0, steps, index_step,
            (jnp.full((1, Q_BLOCK), -1, jnp.int32),
             jnp.full((1, Q_BLOCK), 1, jnp.int32) * (n_keys - 1)))

        def bias_tile(t, carry):
            k0 = tile_start(t)
            s = sc_ref[pl.ds(k0, tk), :]
            keep = jnp.logical_or(
                s > thr, jnp.logical_and(s == thr, key_index(k0) <= last))
            bias_ref[pl.ds(k0, tk), :] = jnp.where(keep, causal_bias(k0), NEG_INF)
            return carry

        lax.fori_loop(0, nk, bias_tile, 0)

    tied = jnp.logical_and(cnt_lo != topk, jnp.logical_not(full_row))
    lax.cond(jnp.max(jnp.where(tied, 1.0, 0.0)) > 0.0,
             write_bias_with_ties, write_bias)

    acc_ref[...] = jnp.zeros(acc_ref.shape, F32)

    def qk_head(t, h, m_old):
        k0 = tile_start(t)
        hs = slice(h * HEAD_DIM, (h + 1) * HEAD_DIM)
        s = lax.dot_general(k_ref[pl.ds(k0, tk), hs], q_ref[:, hs], NT_DIMS,
                            preferred_element_type=F32)
        s = s + bias_ref[pl.ds(k0, tk), :]
        m_new = jnp.maximum(m_old, jnp.max(s, axis=0, keepdims=True))
        p_ref[t % 2, h] = jnp.exp2(s - m_new).astype(BF16)
        return m_new, jnp.exp2(m_old - m_new)

    def pv_head(t, h, alpha):
        pv = jnp.dot(vt_ref[t, h * V_ROWS:(h + 1) * V_ROWS, :], p_ref[t % 2, h],
                     preferred_element_type=F32)
        acc_ref[h] = acc_ref[h] * alpha + pv

    def qk_stage(t, ms):
        pairs = [qk_head(t, h, ms[h]) for h in range(N_HEADS)]
        return tuple(p[0] for p in pairs), tuple(p[1] for p in pairs)

    def pv_stage(t, alphas):
        for h in range(N_HEADS):
            pv_head(t, h, alphas[h])

    def att_tile(t, carry):
        ms, alphas = carry
        pairs = []
        for h in range(N_HEADS):
            pv_head(t - 1, h, alphas[h])
            pairs.append(qk_head(t, h, ms[h]))
        return tuple(p[0] for p in pairs), tuple(p[1] for p in pairs)

    m0 = tuple(jnp.full((1, Q_BLOCK), M_INIT, F32) for _ in range(N_HEADS))
    carry = lax.fori_loop(
        0, (nk - 1) // 2,
        lambda j, c: att_tile(2 * j + 2, att_tile(2 * j + 1, c)),
        qk_stage(0, m0))
    carry = lax.cond((nk - 1) % 2 == 1,
                     lambda c: att_tile(nk - 1, c), lambda c: c, carry)
    pv_stage(nk - 1, carry[1])

    for h in range(N_HEADS):
        hs = slice(h * HEAD_DIM, (h + 1) * HEAD_DIM)
        out_t = acc_ref[h, :HEAD_DIM, :] * (1.0 / acc_ref[h, HEAD_DIM:HEAD_DIM + 1, :])
        gate = ga_ref[:, hs].astype(F32)
        o_ref[:, hs] = (out_t.T * (gate * _sigmoid(gate))).astype(o_ref.dtype)


def _dsa_attention(pr, pv, vt, b, s):
    pr3 = pr.reshape(b, s, pr.shape[1])
    pv3 = pv.reshape(b, s, pv.shape[1])
    topk = min(TOPK_MAX, s // 4)
    w_att = ATTN_WIDTH
    single = pl.Buffered(1)
    return pl.pallas_call(
        functools.partial(_dsa_kernel, topk=topk),
        grid=(b, s // Q_BLOCK),
        in_specs=[
            pl.BlockSpec((None, Q_BLOCK, w_att), lambda bi, i: (bi, i, 0)),
            pl.BlockSpec((None, Q_BLOCK, w_att), lambda bi, i: (bi, i, 2)),
            pl.BlockSpec((None, s, w_att), lambda bi, i: (bi, 0, 1),
                         pipeline_mode=single),
            pl.BlockSpec((None, s, 3 * LANES), lambda bi, i: (bi, 0, 8),
                         pipeline_mode=single),
            pl.BlockSpec((None, s // ATT_TK, N_HEADS * V_ROWS, ATT_TK),
                         lambda bi, i: (bi, 0, 0, 0), pipeline_mode=single),
            pl.BlockSpec((None, Q_BLOCK, w_att), lambda bi, i: (bi, i, 0)),
        ],
        out_specs=pl.BlockSpec((None, Q_BLOCK, w_att), lambda bi, i: (bi, i, 0)),
        out_shape=jax.ShapeDtypeStruct((b, s, w_att), BF16),
        scratch_shapes=[pltpu.VMEM((s, Q_BLOCK), F32),
                        pltpu.VMEM((s, Q_BLOCK), BF16),
                        pltpu.VMEM((s, Q_BLOCK), F32),
                        pltpu.VMEM((N_HEADS, V_ROWS, Q_BLOCK), F32),
                        pltpu.VMEM((2, N_HEADS, ATT_TK, Q_BLOCK), BF16)],
        compiler_params=_cparams(2),
        name="dsa_attention",
    )(pr3, pr3, pr3, pr3, vt, pv3)


RNN_T = 512
N_RNN_TILES = D_RNN // LANES
GATE_WIN = 3


def _gate_window_start(j):
    assert RNN_BLOCK_DIM <= LANES
    return min(max(j - 1, 0), N_RNN_TILES - GATE_WIN)


def _rglru_kernel(xr_ref, gr_ref, cw_ref, cb_ref, wg_ref, bg_ref, lam_ref,
                  o_ref, xp_ref, a_ref, u_ref, h_ref, hc_ref):
    c = pl.program_id(1)
    t_len = RNN_T
    pad = SUBLANES

    @pl.when(c == 0)
    def _():
        xp_ref[0:pad, :] = jnp.zeros((pad, D_RNN), F32)
        hc_ref[...] = jnp.zeros((1, D_RNN), F32)

    x = xr_ref[...].astype(F32)
    xp_ref[pad:pad + t_len, :] = x
    y = cb_ref[...] + cw_ref[CONV_WIDTH - 1:CONV_WIDTH, :] * x
    for j in range(CONV_WIDTH - 1):
        back = CONV_WIDTH - 1 - j
        y = y + cw_ref[j:j + 1, :] * xp_ref[pad - back:pad - back + t_len, :]
    xp_ref[0:pad, :] = x[t_len - pad:, :]

    y16 = y.astype(BF16)
    yh = 0.5 * y
    z = -lam_ref[...]
    softplus = jnp.maximum(z, 0.0) + jnp.log1p(jnp.exp(-jnp.abs(z)))
    k = (-0.5 * LRU_C * LOG2E) * softplus
    for j in range(N_RNN_TILES):
        ls = slice(j * LANES, (j + 1) * LANES)
        w0 = _gate_window_start(j) * LANES
        g = jnp.dot(y16[:, w0:w0 + GATE_WIN * LANES], wg_ref[j],
                    preferred_element_type=F32) + bg_ref[j]
        t_r = jnp.tanh(g[:, :LANES])
        t_i = jnp.tanh(g[:, LANES:])
        a = jnp.exp2(k[:, ls] * t_r + k[:, ls])
        a_ref[:, ls] = a
        om = 1.0 - a * a
        root = jnp.where(om > 0.0, om * lax.rsqrt(om), 0.0)
        u_ref[:, ls] = root * (t_i * yh[:, ls] + yh[:, ls])

    def step(t, h):
        h = a_ref[pl.ds(t, 1), :] * h + u_ref[pl.ds(t, 1), :]
        h_ref[pl.ds(t, 1), :] = h
        return h

    hc_ref[...] = lax.fori_loop(0, t_len, step, hc_ref[...], unroll=8)
    gh = 0.5 * gr_ref[...].astype(F32)
    o_ref[...] = (h_ref[...] * (gh * jnp.tanh(gh) + gh)).astype(o_ref.dtype)


def _rglru(pc, b, s, conv_w, conv_b, w_gates, b_gates, lam):
    pc3 = pc.reshape(b, s, pc.shape[1])
    t = RNN_T
    const = lambda bi, c: (0, 0)
    const3 = lambda bi, c: (0, 0, 0)
    return pl.pallas_call(
        _rglru_kernel,
        grid=(b, s // t),
        in_specs=[
            pl.BlockSpec((None, t, D_RNN), lambda bi, c: (bi, c, 0)),
            pl.BlockSpec((None, t, D_RNN), lambda bi, c: (bi, c, 1)),
            pl.BlockSpec((CONV_WIDTH, D_RNN), const),
            pl.BlockSpec((1, D_RNN), const),
            pl.BlockSpec((N_RNN_TILES, GATE_WIN * LANES, 2 * LANES), const3),
            pl.BlockSpec((N_RNN_TILES, 1, 2 * LANES), const3),
            pl.BlockSpec((1, D_RNN), const),
        ],
        out_specs=pl.BlockSpec((None, t, D_RNN), lambda bi, c: (bi, c, 0)),
        out_shape=jax.ShapeDtypeStruct((b, s, D_RNN), BF16),
        scratch_shapes=[pltpu.VMEM((t + SUBLANES, D_RNN), F32),
                        pltpu.VMEM((t, D_RNN), F32),
                        pltpu.VMEM((t, D_RNN), F32),
                        pltpu.VMEM((t, D_RNN), F32),
                        pltpu.VMEM((1, D_RNN), F32)],
        compiler_params=_cparams(2),
        name="rglru",
    )(pc3, pc3, conv_w, conv_b.reshape(1, D_RNN), w_gates, b_gates,
      lam.reshape(1, D_RNN))


def _pack_gates(w_rg_l, b_rg_l, w_ig_l, b_ig_l):
    dense = [jax.scipy.linalg.block_diag(*w) for w in (w_rg_l, w_ig_l)]
    bands, biases = [], []
    for j in range(N_RNN_TILES):
        w0 = _gate_window_start(j) * LANES
        cols = slice(j * LANES, (j + 1) * LANES)
        bands.append(jnp.concatenate(
            [dm[w0:w0 + GATE_WIN * LANES, cols] for dm in dense], axis=1))
        biases.append(jnp.concatenate([b_rg_l[cols], b_ig_l[cols]])[None, :])
    return (0.5 * jnp.stack(bands)).astype(BF16), 0.5 * jnp.stack(biases)


def _merge_kernel(at_ref, hr_ref, ma_ref, mb_ref, x_ref, woa_ref, wor_ref,
                  wo_ref, g_ref, *out_refs):
    y_a = jnp.dot(at_ref[...], woa_ref[...], preferred_element_type=F32)
    y_b = jnp.dot(hr_ref[...], wor_ref[...], preferred_element_type=F32)
    merged = (_sigmoid(ma_ref[...].astype(F32)) * y_a
              + _sigmoid(mb_ref[...].astype(F32)) * y_b)
    x = x_ref[...] + jnp.dot(merged.astype(BF16), wo_ref[...],
                             preferred_element_type=F32)
    no_ref = out_refs[-1]
    if len(out_refs) == 2:
        out_refs[0][...] = x
    y = x * lax.rsqrt(jnp.mean(x * x, axis=-1, keepdims=True) + NORM_EPS)
    no_ref[...] = (y * g_ref[...]).astype(no_ref.dtype)


def _merge(attn_g, hr_g, pv, x2d, w_oa, w_or, w_o, g, last):
    m, d = x2d.shape
    tm = 1024
    const = lambda i: (0, 0)
    row = pl.BlockSpec((tm, d), lambda i: (i, 0))
    normed = jax.ShapeDtypeStruct((m, d), F32 if last else BF16)
    outs = pl.pallas_call(
        _merge_kernel,
        grid=(m // tm,),
        in_specs=[
            pl.BlockSpec((tm, ATTN_WIDTH), lambda i: (i, 0)),
            pl.BlockSpec((tm, D_RNN), lambda i: (i, 0)),
            pl.BlockSpec((tm, d), lambda i: (i, 1)),
            pl.BlockSpec((tm, d), lambda i: (i, 2)),
            row,
            pl.BlockSpec((ATTN_WIDTH, d), const),
            pl.BlockSpec((D_RNN, d), const),
            pl.BlockSpec((d, d), const),
            pl.BlockSpec((1, d), const),
        ],
        out_specs=[row] if last else [row, row],
        out_shape=[normed] if last else [jax.ShapeDtypeStruct((m, d), F32), normed],
        compiler_params=_cparams(1),
        name="merge",
    )(attn_g, hr_g, pv, pv, x2d, w_oa, w_or, w_o, g.reshape(1, d))
    return (None, outs[0]) if last else tuple(outs)


def _pack_weights(w_in):
    sizes = (ATTN_WIDTH,) * 4 + (IDX_HEADS * IDX_DIM, IDX_DIM, IDX_HEADS,
                                 D_RNN, D_RNN, D_MODEL, D_MODEL)
    (o_q, o_k, o_v, o_ga, o_iq, o_ik, o_iw, o_xr, o_gr, o_ma,
     o_mb, o_end) = np.concatenate([[0], np.cumsum(sizes)]).tolist()
    w16 = w_in.astype(BF16)
    n_l, d = w_in.shape[:2]
    half = IDX_DIM // 2
    z = lambda n: jnp.zeros((n_l, d, n), BF16)
    iq = w16[:, :, o_iq:o_ik].reshape(n_l, d, IDX_HEADS // 2, 2, 2, half).transpose(
        0, 1, 2, 4, 3, 5).reshape(n_l, d, IDX_HEADS * IDX_DIM)
    ik1, ik2 = w16[:, :, o_ik:o_ik + half], w16[:, :, o_ik + half:o_iw]
    w_rope = jnp.concatenate(
        [w16[:, :, o_q:o_v], iq,
         ik1, z(half), ik2, z(half), z(half), ik1, z(half), ik2,
         w16[:, :, o_iw:o_xr], z(ROPE_TN - 2 * LANES - IDX_HEADS)], axis=2)
    w_gate = jnp.concatenate([w16[:, :, o_ga:o_iq], w16[:, :, o_ma:o_end]], axis=2)
    vt = jnp.swapaxes(w16[:, :, o_v:o_ga], 1, 2).reshape(n_l, N_HEADS, HEAD_DIM, d)
    w_vt = jnp.pad(vt, ((0, 0), (0, 0), (0, V_ROWS - HEAD_DIM), (0, 0))).reshape(
        n_l, N_HEADS * V_ROWS, d)
    w_rnn = w16[:, :, o_xr:o_ma]
    return w_rope, w_gate, w_vt, w_rnn


def kernel(x, positions, norm_g, w_in, conv_w, conv_b, w_rg, b_rg, w_ig, b_ig,
           lru_lambda, w_out_attn, w_out_rnn, w_o, final_g):
    b, s, d = x.shape
    depth = w_in.shape[0]
    m = b * s
    cos, sin = _rope_tables(positions)
    x2d = x.reshape(m, d)
    h = _rmsnorm(x2d, norm_g[0], BF16)
    out = None
    w_rope, w_gate, w_vt, w_rnn = _pack_weights(w_in)
    w_oa, w_or, w_oo = (w.astype(BF16) for w in (w_out_attn, w_out_rnn, w_o))
    for l in range(depth):
        w_gates, b_gates = _pack_gates(w_rg[l], b_rg[l], w_ig[l], b_ig[l])
        pr = _proj_rope(h, w_rope[l], cos, sin)
        pv = _proj_plain(h, w_gate[l], 1024, "proj_gate")
        vt = _proj_vt(h, w_vt[l], b, s)
        pc = _proj_plain(h, w_rnn[l], D_RNN, "proj_rnn")
        attn_g = _dsa_attention(pr, pv, vt, b, s).reshape(m, ATTN_WIDTH)
        hr_g = _rglru(pc, b, s, conv_w[l], conv_b[l], w_gates, b_gates,
                      lru_lambda[l]).reshape(m, D_RNN)
        last = l == depth - 1
        g_next = final_g if last else norm_g[l + 1]
        x2d, normed = _merge(attn_g, hr_g, pv, x2d, w_oa[l], w_or[l], w_oo[l],
                             g_next, last)
        if last:
            out = normed
        else:
            h = normed
    return out.reshape(b, s, d)
```

```python
import functools

import jax
import jax.numpy as jnp
import numpy as np
from jax import lax
from jax.experimental import pallas as pl
from jax.experimental.pallas import tpu as pltpu

D_MODEL = 1024
N_HEADS = 8
HEAD_DIM = 128
ATTN_WIDTH = N_HEADS * HEAD_DIM
IDX_HEADS = 16
IDX_DIM = 64
TOPK_MAX = 256
Q_BLOCK = 256
D_RNN = 1408
RNN_BLOCKS = 16
RNN_BLOCK_DIM = D_RNN // RNN_BLOCKS
CONV_WIDTH = 4
LRU_C = 8.0
ROPE_THETA = 10000.0
NORM_EPS = 1e-6

LANES = 128
SUBLANES = 8
PACKED_SUBLANES = 16
VMEM_LIMIT = 56 * 1024 * 1024

F32 = jnp.float32
BF16 = jnp.bfloat16
NEG_INF = float("-inf")
NT_DIMS = (((1,), (1,)), ((), ()))
LOG2E = float(np.log2(np.e))
V_ROWS = HEAD_DIM + SUBLANES


def _sigmoid(x):
    return 0.5 * jnp.tanh(0.5 * x) + 0.5


def _cparams(n_grid):
    return pltpu.CompilerParams(
        dimension_semantics=("arbitrary",) * n_grid,
        vmem_limit_bytes=VMEM_LIMIT)


def _rope_table_kernel(pos_ref, inv_ref, cos_ref, sin_ref):
    ang = pos_ref[...].astype(F32) * inv_ref[...]
    cos, sin = jnp.cos(ang), jnp.sin(ang)
    cos_r, sin_r = pltpu.roll(cos, LANES // 2, 1), pltpu.roll(sin, LANES // 2, 1)
    low = lax.broadcasted_iota(jnp.int32, (1, LANES), 1) < LANES // 2
    cos_ref[:, :LANES] = jnp.where(low, cos, cos_r)
    cos_ref[:, LANES:] = jnp.where(low, cos_r, cos)
    sin_ref[:, :LANES] = jnp.where(low, -sin, sin_r)
    sin_ref[:, LANES:] = jnp.where(low, -sin_r, sin)


def _rope_tables(positions):
    m = positions.size
    inv_a = ROPE_THETA ** (-jnp.arange(0, HEAD_DIM, 2, dtype=F32) / HEAD_DIM)
    inv_i = ROPE_THETA ** (-jnp.arange(0, IDX_DIM, 2, dtype=F32) / IDX_DIM)
    inv = jnp.concatenate([inv_a, inv_i, inv_i])[None, :]
    tm = 2048
    return pl.pallas_call(
        _rope_table_kernel,
        grid=(m // tm,),
        in_specs=[pl.BlockSpec((tm, 1), lambda i: (i, 0)),
                  pl.BlockSpec((1, LANES), lambda i: (0, 0))],
        out_specs=[pl.BlockSpec((tm, 2 * LANES), lambda i: (i, 0))] * 2,
        out_shape=[jax.ShapeDtypeStruct((m, 2 * LANES), F32)] * 2,
        compiler_params=_cparams(1),
        name="rope_tables",
    )(positions.reshape(m, 1), inv)


def _rmsnorm_kernel(x_ref, g_ref, o_ref):
    x = x_ref[...]
    y = x * lax.rsqrt(jnp.mean(x * x, axis=-1, keepdims=True) + NORM_EPS)
    o_ref[...] = (y * g_ref[...]).astype(o_ref.dtype)


def _rmsnorm(x2d, g, out_dtype):
    m, d = x2d.shape
    tm = 1024
    return pl.pallas_call(
        _rmsnorm_kernel,
        grid=(m // tm,),
        in_specs=[pl.BlockSpec((tm, d), lambda i: (i, 0)),
                  pl.BlockSpec((1, d), lambda i: (0, 0))],
        out_specs=pl.BlockSpec((tm, d), lambda i: (i, 0)),
        out_shape=jax.ShapeDtypeStruct((m, d), out_dtype),
        compiler_params=_cparams(1),
        name="rmsnorm",
    )(x2d, g.reshape(1, d))


ROPE_TN = 512


ROPE_TILES = 7


def _rope_epilogue(acc, jp, cos_ref, sin_ref, o_ref):
    cos_a, sin_a = cos_ref[:, :LANES], sin_ref[:, :LANES]
    cos_i, sin_i = cos_ref[:, LANES:], sin_ref[:, LANES:]
    q_scale = (HEAD_DIM ** -0.5) * LOG2E
    use_a = jnp.where(jp < 2, q_scale, jnp.where(jp < 4, 1.0, 0.0)).astype(F32)
    idx_tile = jnp.logical_and(jp >= 4, jp < 6)
    tail = jp == ROPE_TILES - 1
    for c in range(ROPE_TN // LANES):
        ik_lanes = c < 2
        use_i = jnp.where(jnp.logical_or(idx_tile, jnp.logical_and(tail, ik_lanes)),
                          1.0, 0.0).astype(F32)
        ident = jnp.where(jnp.logical_and(tail, not ik_lanes), 1.0, 0.0).astype(F32)
        cos = cos_a * use_a + cos_i * use_i + ident
        sin = sin_a * use_a + sin_i * use_i
        x = acc[:, c * LANES:(c + 1) * LANES]
        o_ref[:, c * LANES:(c + 1) * LANES] = (
            x * cos + pltpu.roll(x, LANES // 2, 1) * sin).astype(o_ref.dtype)


def _proj_rope_kernel(h_ref, w_ref, cos_ref, sin_ref, o_ref, acc0_ref, acc1_ref):
    step = pl.program_id(0)
    jp = jnp.maximum(step - 1, 0) % ROPE_TILES

    @pl.when(step == 0)
    def _():
        acc1_ref[...] = jnp.zeros(acc1_ref.shape, F32)

    def body(acc_new, acc_old):
        acc_new[...] = jnp.dot(h_ref[...], w_ref[...], preferred_element_type=F32)
        _rope_epilogue(acc_old[...], jp, cos_ref, sin_ref, o_ref)

    @pl.when(step % 2 == 0)
    def _():
        body(acc0_ref, acc1_ref)

    @pl.when(step % 2 == 1)
    def _():
        body(acc1_ref, acc0_ref)


def _proj_rope(h, w, cos, sin):
    m, d = h.shape
    n = w.shape[1]
    assert n == ROPE_TILES * ROPE_TN
    tm = 1024
    n_steps = (m // tm) * ROPE_TILES
    cur = lambda s: jnp.minimum(s, n_steps - 1)
    prev = lambda s: jnp.maximum(s - 1, 0)
    return pl.pallas_call(
        _proj_rope_kernel,
        grid=(n_steps + 1,),
        in_specs=[pl.BlockSpec((tm, d), lambda s: (cur(s) // ROPE_TILES, 0)),
                  pl.BlockSpec((d, ROPE_TN), lambda s: (0, cur(s) % ROPE_TILES)),
                  pl.BlockSpec((tm, 2 * LANES), lambda s: (prev(s) // ROPE_TILES, 0)),
                  pl.BlockSpec((tm, 2 * LANES), lambda s: (prev(s) // ROPE_TILES, 0))],
        out_specs=pl.BlockSpec(
            (tm, ROPE_TN), lambda s: (prev(s) // ROPE_TILES, prev(s) % ROPE_TILES)),
        out_shape=jax.ShapeDtypeStruct((m, n), BF16),
        scratch_shapes=[pltpu.VMEM((tm, ROPE_TN), F32)] * 2,
        compiler_params=_cparams(1),
        name="proj_rope",
    )(h, w, cos, sin)


def _proj_plain_kernel(h_ref, w_ref, o_ref):
    o_ref[...] = jnp.dot(h_ref[...], w_ref[...],
                         preferred_element_type=F32).astype(o_ref.dtype)


def _proj_plain(h, w, tm, tn, name):
    m, d = h.shape
    n = w.shape[1]
    return pl.pallas_call(
        _proj_plain_kernel,
        grid=(m // tm, n // tn),
        in_specs=[pl.BlockSpec((tm, d), lambda i, j: (i, 0)),
                  pl.BlockSpec((d, tn), lambda i, j: (0, j))],
        out_specs=pl.BlockSpec((tm, tn), lambda i, j: (i, j)),
        out_shape=jax.ShapeDtypeStruct((m, n), BF16),
        compiler_params=_cparams(2),
        name=name,
    )(h, w)


ATT_TK = 256


VT_TILES = 4


def _proj_vt_kernel(h_ref, wt_ref, o_ref):
    vt = lax.dot_general(wt_ref[...], h_ref[...], NT_DIMS,
                         preferred_element_type=F32)
    row = lax.broadcasted_iota(jnp.int32, (vt.shape[0], 1), 0)
    vt = jnp.where(row % V_ROWS == HEAD_DIM, 1.0, vt).astype(o_ref.dtype)
    for j in range(VT_TILES):
        o_ref[j] = vt[:, j * ATT_TK:(j + 1) * ATT_TK]


def _proj_vt(h, wt, b, s):
    d = h.shape[1]
    n = wt.shape[0]
    tk = ATT_TK
    rows = VT_TILES * tk
    return pl.pallas_call(
        _proj_vt_kernel,
        grid=(b, s // rows),
        in_specs=[pl.BlockSpec((None, rows, d), lambda bi, t: (bi, t, 0)),
                  pl.BlockSpec((n, d), lambda bi, t: (0, 0))],
        out_specs=pl.BlockSpec((None, VT_TILES, n, tk), lambda bi, t: (bi, t, 0, 0)),
        out_shape=jax.ShapeDtypeStruct((b, s // tk, n, tk), BF16),
        compiler_params=_cparams(2),
        name="proj_vt",
    )(h.reshape(b, s, d), wt)


SCORE_TK = 512
COARSE_BITS = 16
BF16_STEP = 1 << 16
FINE_UNCHECKED = 10
N_CHAINS = 4
M_INIT = -1e30


def _key16_to_f32(u):
    key = u ^ 0x8000
    bits = jnp.where((key & 0x8000) != 0, key ^ 0x7FFF, key)
    return pltpu.bitcast(jnp.left_shift(bits, 16), F32)


def _f32_to_key32(x):
    bits = pltpu.bitcast(x, jnp.int32)
    return jnp.where(bits < 0, bits ^ 0x7FFFFFFF, bits)


def _key32_to_f32(key):
    return pltpu.bitcast(jnp.where(key < 0, key ^ 0x7FFFFFFF, key), F32)


def _dsa_kernel(q_ref, iq_ref, k_ref, ikw_ref, vt_ref, ga_ref, o_ref,
                sc_ref, sc16_ref, bias_ref, acc_ref, p_ref, *, topk):
    i = pl.program_id(1)
    tk = ATT_TK
    nk = ((i + 1) * Q_BLOCK + tk - 1) // tk
    q0 = pl.multiple_of(i * Q_BLOCK, Q_BLOCK)
    qidx = q0 + lax.broadcasted_iota(jnp.int32, (1, Q_BLOCK), 1)

    def tile_start(t):
        return pl.multiple_of(t * tk, tk)

    def key_index(k0):
        return k0 + lax.broadcasted_iota(jnp.int32, (tk, 1), 0)

    iq = iq_ref[...]
    n_pair = IDX_HEADS // 2
    lhs = jnp.concatenate(
        [iq[:, p * LANES:(p + 1) * LANES] for p in range(n_pair)], axis=0)
    w = ikw_ref[pl.ds(q0, Q_BLOCK), 2 * LANES:3 * LANES].astype(F32)
    wt = w.T * ((IDX_HEADS ** -0.5) * (IDX_DIM ** -0.5))
    wrow = [wt[h:h + 1, :] for h in range(IDX_HEADS)]

    def score_tile(t, carry):
        k0 = pl.multiple_of(t * SCORE_TK, SCORE_TK)
        acc = jnp.zeros((SCORE_TK, Q_BLOCK), F32)
        for e in range(2):
            ik_e = ikw_ref[pl.ds(k0, SCORE_TK), e * LANES:(e + 1) * LANES]
            d = lax.dot_general(ik_e, lhs, NT_DIMS,
                                preferred_element_type=F32)
            for p in range(n_pair):
                dp = d[:, p * Q_BLOCK:(p + 1) * Q_BLOCK]
                acc = acc + wrow[2 * p + e] * jnp.maximum(dp, 0.0)
        kidx = k0 + lax.broadcasted_iota(jnp.int32, (SCORE_TK, 1), 0)
        sc = jnp.where(kidx <= qidx, acc, NEG_INF)
        sc_ref[pl.ds(k0, SCORE_TK), :] = sc
        sc16_ref[pl.ds(k0, SCORE_TK), :] = sc.astype(BF16)
        return carry

    lax.fori_loop(0, ((i + 1) * Q_BLOCK + SCORE_TK - 1) // SCORE_TK, score_tile, 0)

    one16 = jnp.ones((), BF16)
    zero16 = jnp.zeros((), BF16)
    n_packed = tk // PACKED_SUBLANES

    def coarse(it, t_u):
        cand_u = t_u | jnp.left_shift(jnp.int32(1), COARSE_BITS - 1 - it)
        cand = _key16_to_f32(cand_u).astype(BF16)

        def count_tile(t, cnts):
            x = sc16_ref[pl.ds(tile_start(t), tk), :]
            ind = jnp.where(x >= cand, one16, zero16)
            cnts = list(cnts)
            for j in range(n_packed):
                cnts[j % N_CHAINS] = cnts[j % N_CHAINS] + ind[
                    j * PACKED_SUBLANES:(j + 1) * PACKED_SUBLANES, :]
            return tuple(cnts)

        cnts = lax.fori_loop(
            0, nk, count_tile,
            (jnp.zeros((PACKED_SUBLANES, Q_BLOCK), BF16),) * N_CHAINS)
        total = jnp.sum(sum(c.astype(F32) for c in cnts), axis=0, keepdims=True)
        return jnp.where(total >= topk, cand_u, t_u)

    t_u = lax.fori_loop(0, COARSE_BITS, coarse,
                        jnp.zeros((1, Q_BLOCK), jnp.int32))

    def count_ge(cand):
        def count_tile(t, cnts):
            s = sc_ref[pl.ds(tile_start(t), tk), :]
            ind = jnp.where(s >= cand, 1.0, 0.0)
            cnts = list(cnts)
            for j in range(tk // SUBLANES):
                cnts[j % N_CHAINS] = cnts[j % N_CHAINS] + ind[
                    j * SUBLANES:(j + 1) * SUBLANES, :]
            return tuple(cnts)

        cnts = lax.fori_loop(
            0, nk, count_tile,
            (jnp.zeros((SUBLANES, Q_BLOCK), F32),) * N_CHAINS)
        return jnp.sum(sum(cnts), axis=0, keepdims=True)

    base = _f32_to_key32(_key16_to_f32(t_u))
    lo0 = base - BF16_STEP // 2
    hi0 = base + BF16_STEP + 1

    def fine_cond(st):
        lo, hi, cnt_lo = st
        open_ = jnp.logical_and(cnt_lo != topk, hi - lo > 1)
        return jnp.max(jnp.where(open_, 1.0, 0.0)) > 0.0

    def fine_step(st):
        lo, hi, cnt_lo = st
        mid = lo + jnp.right_shift(hi - lo, 1)
        cnt = count_ge(_key32_to_f32(mid))
        ge = cnt >= topk
        return (jnp.where(ge, mid, lo), jnp.where(ge, hi, mid),
                jnp.where(ge, cnt, cnt_lo))

    st = (lo0, hi0, jnp.full((1, Q_BLOCK), -1.0, F32))
    st = lax.fori_loop(0, FINE_UNCHECKED, lambda _, s_: fine_step(s_), st)
    lo, _, cnt_lo = lax.while_loop(
        fine_cond, lambda s_: fine_step(fine_step(s_)), st)
    full_row = qidx < topk
    thr = jnp.where(full_row, NEG_INF, _key32_to_f32(lo))

    def causal_bias(k0):
        return jnp.where(key_index(k0) <= qidx, 0.0, NEG_INF)

    def write_bias():
        def bias_tile(t, carry):
            k0 = tile_start(t)
            s = sc_ref[pl.ds(k0, tk), :]
            bias_ref[pl.ds(k0, tk), :] = jnp.where(s >= thr, causal_bias(k0), NEG_INF)
            return carry

        lax.fori_loop(0, nk, bias_tile, 0)

    def write_bias_with_ties():
        def count_where(pred):
            def count_tile(t, cnt):
                k0 = tile_start(t)
                ind = jnp.where(pred(sc_ref[pl.ds(k0, tk), :], key_index(k0)), 1.0, 0.0)
                return cnt + jnp.sum(ind, axis=0, keepdims=True)

            return lax.fori_loop(0, nk, count_tile, jnp.zeros((1, Q_BLOCK), F32))

        n_tied = topk - count_where(lambda s, kidx: s > thr)

        def index_step(_, st_):
            lo_i, hi_i = st_
            mid = lo_i + jnp.right_shift(hi_i - lo_i, 1)
            cnt = count_where(
                lambda s, kidx: jnp.logical_and(s == thr, kidx <= mid))
            ge = cnt >= n_tied
            return jnp.where(ge, lo_i, mid), jnp.where(ge, mid, hi_i)

        n_keys = nk * tk
        steps = int(np.ceil(np.log2(sc_ref.shape[0]))) + 1
        _, last = lax.fori_loop(
            0, steps, index_step,
            (jnp.full((1, Q_BLOCK), -1, jnp.int32),
             jnp.full((1, Q_BLOCK), 1, jnp.int32) * (n_keys - 1)))

        def bias_tile(t, carry):
            k0 = tile_start(t)
            s = sc_ref[pl.ds(k0, tk), :]
            keep = jnp.logical_or(
                s > thr, jnp.logical_and(s == thr, key_index(k0) <= last))
            bias_ref[pl.ds(k0, tk), :] = jnp.where(keep, causal_bias(k0), NEG_INF)
            return carry

        lax.fori_loop(0, nk, bias_tile, 0)

    tied = jnp.logical_and(cnt_lo != topk, jnp.logical_not(full_row))
    lax.cond(jnp.max(jnp.where(tied, 1.0, 0.0)) > 0.0,
             write_bias_with_ties, write_bias)

    acc_ref[...] = jnp.zeros(acc_ref.shape, F32)

    def qk_head(t, h, m_old):
        k0 = tile_start(t)
        hs = slice(h * HEAD_DIM, (h + 1) * HEAD_DIM)
        s = lax.dot_general(k_ref[pl.ds(k0, tk), hs], q_ref[:, hs], NT_DIMS,
                            preferred_element_type=F32)
        s = s + bias_ref[pl.ds(k0, tk), :]
        m_new = jnp.maximum(m_old, jnp.max(s, axis=0, keepdims=True))
        p_ref[t % 2, h] = jnp.exp2(s - m_new).astype(BF16)
        return m_new, jnp.exp2(m_old - m_new)

    def pv_head(t, h, alpha):
        pv = jnp.dot(vt_ref[t, h * V_ROWS:(h + 1) * V_ROWS, :], p_ref[t % 2, h],
                     preferred_element_type=F32)
        acc_ref[h] = acc_ref[h] * alpha + pv

    def qk_stage(t, ms):
        pairs = [qk_head(t, h, ms[h]) for h in range(N_HEADS)]
        return tuple(p[0] for p in pairs), tuple(p[1] for p in pairs)

    def pv_stage(t, alphas):
        for h in range(N_HEADS):
            pv_head(t, h, alphas[h])

    def att_tile(t, carry):
        ms, alphas = carry
        pairs = []
        for h in range(N_HEADS):
            pv_head(t - 1, h, alphas[h])
            pairs.append(qk_head(t, h, ms[h]))
        return tuple(p[0] for p in pairs), tuple(p[1] for p in pairs)

    m0 = tuple(jnp.full((1, Q_BLOCK), M_INIT, F32) for _ in range(N_HEADS))
    carry = lax.fori_loop(
        0, (nk - 1) // 2,
        lambda j, c: att_tile(2 * j + 2, att_tile(2 * j + 1, c)),
        qk_stage(0, m0))
    carry = lax.cond((nk - 1) % 2 == 1,
                     lambda c: att_tile(nk - 1, c), lambda c: c, carry)
    pv_stage(nk - 1, carry[1])

    for h in range(N_HEADS):
        hs = slice(h * HEAD_DIM, (h + 1) * HEAD_DIM)
        out_t = acc_ref[h, :HEAD_DIM, :] * (1.0 / acc_ref[h, HEAD_DIM:HEAD_DIM + 1, :])
        gate = ga_ref[:, hs].astype(F32)
        o_ref[:, hs] = (out_t.T * (gate * _sigmoid(gate))).astype(o_ref.dtype)


def _dsa_attention(pr, pv, vt, b, s):
    pr3 = pr.reshape(b, s, pr.shape[1])
    pv3 = pv.reshape(b, s, pv.shape[1])
    topk = min(TOPK_MAX, s // 4)
    w_att = ATTN_WIDTH
    single = pl.Buffered(1)
    return pl.pallas_call(
        functools.partial(_dsa_kernel, topk=topk),
        grid=(b, s // Q_BLOCK),
        in_specs=[
            pl.BlockSpec((None, Q_BLOCK, w_att), lambda bi, i: (bi, i, 0)),
            pl.BlockSpec((None, Q_BLOCK, w_att), lambda bi, i: (bi, i, 2)),
            pl.BlockSpec((None, s, w_att), lambda bi, i: (bi, 0, 1),
                         pipeline_mode=single),
            pl.BlockSpec((None, s, 3 * LANES), lambda bi, i: (bi, 0, 8),
                         pipeline_mode=single),
            pl.BlockSpec((None, s // ATT_TK, N_HEADS * V_ROWS, ATT_TK),
                         lambda bi, i: (bi, 0, 0, 0), pipeline_mode=single),
            pl.BlockSpec((None, Q_BLOCK, w_att), lambda bi, i: (bi, i, 0)),
        ],
        out_specs=pl.BlockSpec((None, Q_BLOCK, w_att), lambda bi, i: (bi, i, 0)),
        out_shape=jax.ShapeDtypeStruct((b, s, w_att), BF16),
        scratch_shapes=[pltpu.VMEM((s, Q_BLOCK), F32),
                        pltpu.VMEM((s, Q_BLOCK), BF16),
                        pltpu.VMEM((s, Q_BLOCK), F32),
                        pltpu.VMEM((N_HEADS, V_ROWS, Q_BLOCK), F32),
                        pltpu.VMEM((2, N_HEADS, ATT_TK, Q_BLOCK), BF16)],
        compiler_params=_cparams(2),
        name="dsa_attention",
    )(pr3, pr3, pr3, pr3, vt, pv3)


RNN_T = 512
N_RNN_TILES = D_RNN // LANES
GATE_WIN = 3


def _gate_window_start(j):
    assert RNN_BLOCK_DIM <= LANES
    return min(max(j - 1, 0), N_RNN_TILES - GATE_WIN)


def _rglru_kernel(xr_ref, gr_ref, cw_ref, cb_ref, wg_ref, bg_ref, lam_ref,
                  o_ref, xp_ref, a_ref, u_ref, h_ref, hc_ref):
    c = pl.program_id(1)
    t_len = RNN_T
    pad = SUBLANES

    @pl.when(c == 0)
    def _():
        xp_ref[0:pad, :] = jnp.zeros((pad, D_RNN), F32)
        hc_ref[...] = jnp.zeros((1, D_RNN), F32)

    x = xr_ref[...].astype(F32)
    xp_ref[pad:pad + t_len, :] = x
    y = cb_ref[...] + cw_ref[CONV_WIDTH - 1:CONV_WIDTH, :] * x
    for j in range(CONV_WIDTH - 1):
        back = CONV_WIDTH - 1 - j
        y = y + cw_ref[j:j + 1, :] * xp_ref[pad - back:pad - back + t_len, :]
    xp_ref[0:pad, :] = x[t_len - pad:, :]

    y16 = y.astype(BF16)
    yh = 0.5 * y
    z = -lam_ref[...]
    softplus = jnp.maximum(z, 0.0) + jnp.log1p(jnp.exp(-jnp.abs(z)))
    k = (-0.5 * LRU_C * LOG2E) * softplus
    for j in range(N_RNN_TILES):
        ls = slice(j * LANES, (j + 1) * LANES)
        w0 = _gate_window_start(j) * LANES
        g = jnp.dot(y16[:, w0:w0 + GATE_WIN * LANES], wg_ref[j],
                    preferred_element_type=F32) + bg_ref[j]
        t_r = jnp.tanh(g[:, :LANES])
        t_i = jnp.tanh(g[:, LANES:])
        a = jnp.exp2(k[:, ls] * t_r + k[:, ls])
        a_ref[:, ls] = a
        om = 1.0 - a * a
        root = jnp.where(om > 0.0, om * lax.rsqrt(om), 0.0)
        u_ref[:, ls] = root * (t_i * yh[:, ls] + yh[:, ls])

    def step(t, h):
        h = a_ref[pl.ds(t, 1), :] * h + u_ref[pl.ds(t, 1), :]
        h_ref[pl.ds(t, 1), :] = h
        return h

    hc_ref[...] = lax.fori_loop(0, t_len, step, hc_ref[...], unroll=8)
    gh = 0.5 * gr_ref[...].astype(F32)
    o_ref[...] = (h_ref[...] * (gh * jnp.tanh(gh) + gh)).astype(o_ref.dtype)


def _rglru(pc, b, s, conv_w, conv_b, w_gates, b_gates, lam):
    pc3 = pc.reshape(b, s, pc.shape[1])
    t = RNN_T
    const = lambda bi, c: (0, 0)
    const3 = lambda bi, c: (0, 0, 0)
    return pl.pallas_call(
        _rglru_kernel,
        grid=(b, s // t),
        in_specs=[
            pl.BlockSpec((None, t, D_RNN), lambda bi, c: (bi, c, 0)),
            pl.BlockSpec((None, t, D_RNN), lambda bi, c: (bi, c, 1)),
            pl.BlockSpec((CONV_WIDTH, D_RNN), const),
            pl.BlockSpec((1, D_RNN), const),
            pl.BlockSpec((N_RNN_TILES, GATE_WIN * LANES, 2 * LANES), const3),
            pl.BlockSpec((N_RNN_TILES, 1, 2 * LANES), const3),
            pl.BlockSpec((1, D_RNN), const),
        ],
        out_specs=pl.BlockSpec((None, t, D_RNN), lambda bi, c: (bi, c, 0)),
        out_shape=jax.ShapeDtypeStruct((b, s, D_RNN), BF16),
        scratch_shapes=[pltpu.VMEM((t + SUBLANES, D_RNN), F32),
                        pltpu.VMEM((t, D_RNN), F32),
                        pltpu.VMEM((t, D_RNN), F32),
                        pltpu.VMEM((t, D_RNN), F32),
                        pltpu.VMEM((1, D_RNN), F32)],
        compiler_params=_cparams(2),
        name="rglru",
    )(pc3, pc3, conv_w, conv_b.reshape(1, D_RNN), w_gates, b_gates,
      lam.reshape(1, D_RNN))


def _pack_gates(w_rg_l, b_rg_l, w_ig_l, b_ig_l):
    dense = [jax.scipy.linalg.block_diag(*w) for w in (w_rg_l, w_ig_l)]
    bands, biases = [], []
    for j in range(N_RNN_TILES):
        w0 = _gate_window_start(j) * LANES
        cols = slice(j * LANES, (j + 1) * LANES)
        bands.append(jnp.concatenate(
            [dm[w0:w0 + GATE_WIN * LANES, cols] for dm in dense], axis=1))
        biases.append(jnp.concatenate([b_rg_l[cols], b_ig_l[cols]])[None, :])
    return (0.5 * jnp.stack(bands)).astype(BF16), 0.5 * jnp.stack(biases)


def _merge_kernel(at_ref, hr_ref, ma_ref, mb_ref, x_ref, woa_ref, wor_ref,
                  wo_ref, g_ref, *out_refs):
    y_a = jnp.dot(at_ref[...], woa_ref[...], preferred_element_type=F32)
    y_b = jnp.dot(hr_ref[...], wor_ref[...], preferred_element_type=F32)
    merged = (_sigmoid(ma_ref[...].astype(F32)) * y_a
              + _sigmoid(mb_ref[...].astype(F32)) * y_b)
    x = x_ref[...] + jnp.dot(merged.astype(BF16), wo_ref[...],
                             preferred_element_type=F32)
    no_ref = out_refs[-1]
    if len(out_refs) == 2:
        out_refs[0][...] = x
    y = x * lax.rsqrt(jnp.mean(x * x, axis=-1, keepdims=True) + NORM_EPS)
    no_ref[...] = (y * g_ref[...]).astype(no_ref.dtype)


def _merge(attn_g, hr_g, pv, x2d, w_oa, w_or, w_o, g, last):
    m, d = x2d.shape
    tm = 1024
    const = lambda i: (0, 0)
    row = pl.BlockSpec((tm, d), lambda i: (i, 0))
    normed = jax.ShapeDtypeStruct((m, d), F32 if last else BF16)
    outs = pl.pallas_call(
        _merge_kernel,
        grid=(m // tm,),
        in_specs=[
            pl.BlockSpec((tm, ATTN_WIDTH), lambda i: (i, 0)),
            pl.BlockSpec((tm, D_RNN), lambda i: (i, 0)),
            pl.BlockSpec((tm, d), lambda i: (i, 1)),
            pl.BlockSpec((tm, d), lambda i: (i, 2)),
            row,
            pl.BlockSpec((ATTN_WIDTH, d), const),
            pl.BlockSpec((D_RNN, d), const),
            pl.BlockSpec((d, d), const),
            pl.BlockSpec((1, d), const),
        ],
        out_specs=[row] if last else [row, row],
        out_shape=[normed] if last else [jax.ShapeDtypeStruct((m, d), F32), normed],
        compiler_params=_cparams(1),
        name="merge",
    )(attn_g, hr_g, pv, pv, x2d, w_oa, w_or, w_o, g.reshape(1, d))
    return (None, outs[0]) if last else tuple(outs)


def _pack_weights(w_in):
    sizes = (ATTN_WIDTH,) * 4 + (IDX_HEADS * IDX_DIM, IDX_DIM, IDX_HEADS,
                                 D_RNN, D_RNN, D_MODEL, D_MODEL)
    (o_q, o_k, o_v, o_ga, o_iq, o_ik, o_iw, o_xr, o_gr, o_ma,
     o_mb, o_end) = np.concatenate([[0], np.cumsum(sizes)]).tolist()
    w16 = w_in.astype(BF16)
    n_l, d = w_in.shape[:2]
    half = IDX_DIM // 2
    z = lambda n: jnp.zeros((n_l, d, n), BF16)
    iq = w16[:, :, o_iq:o_ik].reshape(n_l, d, IDX_HEADS // 2, 2, 2, half).transpose(
        0, 1, 2, 4, 3, 5).reshape(n_l, d, IDX_HEADS * IDX_DIM)
    ik1, ik2 = w16[:, :, o_ik:o_ik + half], w16[:, :, o_ik + half:o_iw]
    w_rope = jnp.concatenate(
        [w16[:, :, o_q:o_v], iq,
         ik1, z(half), ik2, z(half), z(half), ik1, z(half), ik2,
         w16[:, :, o_iw:o_xr], z(ROPE_TN - 2 * LANES - IDX_HEADS)], axis=2)
    w_gate = jnp.concatenate([w16[:, :, o_ga:o_iq], w16[:, :, o_ma:o_end]], axis=2)
    vt = jnp.swapaxes(w16[:, :, o_v:o_ga], 1, 2).reshape(n_l, N_HEADS, HEAD_DIM, d)
    w_vt = jnp.pad(vt, ((0, 0), (0, 0), (0, V_ROWS - HEAD_DIM), (0, 0))).reshape(
        n_l, N_HEADS * V_ROWS, d)
    w_rnn = w16[:, :, o_xr:o_ma]
    return w_rope, w_gate, w_vt, w_rnn


def kernel(x, positions, norm_g, w_in, conv_w, conv_b, w_rg, b_rg, w_ig, b_ig,
           lru_lambda, w_out_attn, w_out_rnn, w_o, final_g):
    b, s, d = x.shape
    depth = w_in.shape[0]
    m = b * s
    cos, sin = _rope_tables(positions)
    x2d = x.reshape(m, d)
    h = _rmsnorm(x2d, norm_g[0], BF16)
    out = None
    w_rope, w_gate, w_vt, w_rnn = _pack_weights(w_in)
    w_oa, w_or, w_oo = (w.astype(BF16) for w in (w_out_attn, w_out_rnn, w_o))
    for l in range(depth):
        w_gates, b_gates = _pack_gates(w_rg[l], b_rg[l], w_ig[l], b_ig[l])
        pr = _proj_rope(h, w_rope[l], cos, sin)
        pv = _proj_plain(h, w_gate[l], 4096, 1024, "proj_gate")
        vt = _proj_vt(h, w_vt[l], b, s)
        pc = _proj_plain(h, w_rnn[l], 2048, D_RNN, "proj_rnn")
        attn_g = _dsa_attention(pr, pv, vt, b, s).reshape(m, ATTN_WIDTH)
        hr_g = _rglru(pc, b, s, conv_w[l], conv_b[l], w_gates, b_gates,
                      lru_lambda[l]).reshape(m, D_RNN)
        last = l == depth - 1
        g_next = final_g if last else norm_g[l + 1]
        x2d, normed = _merge(attn_g, hr_g, pv, x2d, w_oa[l], w_or[l], w_oo[l],
                             g_next, last)
        if last:
            out = normed
        else:
            h = normed
    return out.reshape(b, s, d)
```

```python
import functools

import jax
import jax.numpy as jnp
import numpy as np
from jax import lax
from jax.experimental import pallas as pl
from jax.experimental.pallas import tpu as pltpu

D_MODEL = 1024
N_HEADS = 8
HEAD_DIM = 128
ATTN_WIDTH = N_HEADS * HEAD_DIM
IDX_HEADS = 16
IDX_DIM = 64
TOPK_MAX = 256
Q_BLOCK = 256
D_RNN = 1408
RNN_BLOCKS = 16
RNN_BLOCK_DIM = D_RNN // RNN_BLOCKS
CONV_WIDTH = 4
LRU_C = 8.0
ROPE_THETA = 10000.0
NORM_EPS = 1e-6

LANES = 128
SUBLANES = 8
PACKED_SUBLANES = 16
VMEM_LIMIT = 56 * 1024 * 1024

F32 = jnp.float32
BF16 = jnp.bfloat16
NEG_INF = float("-inf")
NT_DIMS = (((1,), (1,)), ((), ()))
LOG2E = float(np.log2(np.e))
V_ROWS = HEAD_DIM + SUBLANES


def _sigmoid(x):
    return 0.5 * jnp.tanh(0.5 * x) + 0.5


def _cparams(n_grid):
    return pltpu.CompilerParams(
        dimension_semantics=("arbitrary",) * n_grid,
        vmem_limit_bytes=VMEM_LIMIT)


def _rope_table_kernel(pos_ref, inv_ref, cos_ref, sin_ref):
    ang = pos_ref[...].astype(F32) * inv_ref[...]
    cos, sin = jnp.cos(ang), jnp.sin(ang)
    cos_r, sin_r = pltpu.roll(cos, LANES // 2, 1), pltpu.roll(sin, LANES // 2, 1)
    low = lax.broadcasted_iota(jnp.int32, (1, LANES), 1) < LANES // 2
    cos_ref[:, :LANES] = jnp.where(low, cos, cos_r)
    cos_ref[:, LANES:] = jnp.where(low, cos_r, cos)
    sin_ref[:, :LANES] = jnp.where(low, -sin, sin_r)
    sin_ref[:, LANES:] = jnp.where(low, -sin_r, sin)


def _rope_tables(positions):
    m = positions.size
    inv_a = ROPE_THETA ** (-jnp.arange(0, HEAD_DIM, 2, dtype=F32) / HEAD_DIM)
    inv_i = ROPE_THETA ** (-jnp.arange(0, IDX_DIM, 2, dtype=F32) / IDX_DIM)
    inv = jnp.concatenate([inv_a, inv_i, inv_i])[None, :]
    tm = 2048
    return pl.pallas_call(
        _rope_table_kernel,
        grid=(m // tm,),
        in_specs=[pl.BlockSpec((tm, 1), lambda i: (i, 0)),
                  pl.BlockSpec((1, LANES), lambda i: (0, 0))],
        out_specs=[pl.BlockSpec((tm, 2 * LANES), lambda i: (i, 0))] * 2,
        out_shape=[jax.ShapeDtypeStruct((m, 2 * LANES), F32)] * 2,
        compiler_params=_cparams(1),
        name="rope_tables",
    )(positions.reshape(m, 1), inv)


def _rmsnorm_kernel(x_ref, g_ref, o_ref):
    x = x_ref[...]
    y = x * lax.rsqrt(jnp.mean(x * x, axis=-1, keepdims=True) + NORM_EPS)
    o_ref[...] = (y * g_ref[...]).astype(o_ref.dtype)


def _rmsnorm(x2d, g, out_dtype):
    m, d = x2d.shape
    tm = 1024
    return pl.pallas_call(
        _rmsnorm_kernel,
        grid=(m // tm,),
        in_specs=[pl.BlockSpec((tm, d), lambda i: (i, 0)),
                  pl.BlockSpec((1, d), lambda i: (0, 0))],
        out_specs=pl.BlockSpec((tm, d), lambda i: (i, 0)),
        out_shape=jax.ShapeDtypeStruct((m, d), out_dtype),
        compiler_params=_cparams(1),
        name="rmsnorm",
    )(x2d, g.reshape(1, d))


ROPE_TN = 512


ROPE_TILES = 7


def _rope_epilogue(acc, jp, cos_ref, sin_ref, o_ref):
    cos_a, sin_a = cos_ref[:, :LANES], sin_ref[:, :LANES]
    cos_i, sin_i = cos_ref[:, LANES:], sin_ref[:, LANES:]
    q_scale = (HEAD_DIM ** -0.5) * LOG2E
    use_a = jnp.where(jp < 2, q_scale, jnp.where(jp < 4, 1.0, 0.0)).astype(F32)
    idx_tile = jnp.logical_and(jp >= 4, jp < 6)
    tail = jp == ROPE_TILES - 1
    for c in range(ROPE_TN // LANES):
        ik_lanes = c < 2
        use_i = jnp.where(jnp.logical_or(idx_tile, jnp.logical_and(tail, ik_lanes)),
                          1.0, 0.0).astype(F32)
        ident = jnp.where(jnp.logical_and(tail, not ik_lanes), 1.0, 0.0).astype(F32)
        cos = cos_a * use_a + cos_i * use_i + ident
        sin = sin_a * use_a + sin_i * use_i
        x = acc[:, c * LANES:(c + 1) * LANES]
        o_ref[:, c * LANES:(c + 1) * LANES] = (
            x * cos + pltpu.roll(x, LANES // 2, 1) * sin).astype(o_ref.dtype)


def _proj_rope_kernel(h_ref, w_ref, cos_ref, sin_ref, o_ref, acc0_ref, acc1_ref):
    step = pl.program_id(0)
    jp = jnp.maximum(step - 1, 0) % ROPE_TILES

    @pl.when(step == 0)
    def _():
        acc1_ref[...] = jnp.zeros(acc1_ref.shape, F32)

    def body(acc_new, acc_old):
        acc_new[...] = jnp.dot(h_ref[...], w_ref[...], preferred_element_type=F32)
        _rope_epilogue(acc_old[...], jp, cos_ref, sin_ref, o_ref)

    @pl.when(step % 2 == 0)
    def _():
        body(acc0_ref, acc1_ref)

    @pl.when(step % 2 == 1)
    def _():
        body(acc1_ref, acc0_ref)


def _proj_rope(h, w, cos, sin):
    m, d = h.shape
    n = w.shape[1]
    assert n == ROPE_TILES * ROPE_TN
    tm = 1024
    n_steps = (m // tm) * ROPE_TILES
    cur = lambda s: jnp.minimum(s, n_steps - 1)
    prev = lambda s: jnp.maximum(s - 1, 0)
    return pl.pallas_call(
        _proj_rope_kernel,
        grid=(n_steps + 1,),
        in_specs=[pl.BlockSpec((tm, d), lambda s: (cur(s) // ROPE_TILES, 0)),
                  pl.BlockSpec((d, ROPE_TN), lambda s: (0, cur(s) % ROPE_TILES)),
                  pl.BlockSpec((tm, 2 * LANES), lambda s: (prev(s) // ROPE_TILES, 0)),
                  pl.BlockSpec((tm, 2 * LANES), lambda s: (prev(s) // ROPE_TILES, 0))],
        out_specs=pl.BlockSpec(
            (tm, ROPE_TN), lambda s: (prev(s) // ROPE_TILES, prev(s) % ROPE_TILES)),
        out_shape=jax.ShapeDtypeStruct((m, n), BF16),
        scratch_shapes=[pltpu.VMEM((tm, ROPE_TN), F32)] * 2,
        compiler_params=_cparams(1),
        name="proj_rope",
    )(h, w, cos, sin)


def _proj_plain_kernel(h_ref, w_ref, o_ref):
    o_ref[...] = jnp.dot(h_ref[...], w_ref[...],
                         preferred_element_type=F32).astype(o_ref.dtype)


def _proj_plain(h, w, tm, tn, name):
    m, d = h.shape
    n = w.shape[1]
    return pl.pallas_call(
        _proj_plain_kernel,
        grid=(m // tm, n // tn),
        in_specs=[pl.BlockSpec((tm, d), lambda i, j: (i, 0)),
                  pl.BlockSpec((d, tn), lambda i, j: (0, j))],
        out_specs=pl.BlockSpec((tm, tn), lambda i, j: (i, j)),
        out_shape=jax.ShapeDtypeStruct((m, n), BF16),
        compiler_params=_cparams(2),
        name=name,
    )(h, w)


ATT_TK = 256


VT_TILES = 4


def _proj_vt_kernel(h_ref, wt_ref, o_ref):
    vt = lax.dot_general(wt_ref[...], h_ref[...], NT_DIMS,
                         preferred_element_type=F32)
    row = lax.broadcasted_iota(jnp.int32, (vt.shape[0], 1), 0)
    vt = jnp.where(row % V_ROWS == HEAD_DIM, 1.0, vt).astype(o_ref.dtype)
    for j in range(VT_TILES):
        o_ref[j] = vt[:, j * ATT_TK:(j + 1) * ATT_TK]


def _proj_vt(h, wt, b, s):
    d = h.shape[1]
    n = wt.shape[0]
    tk = ATT_TK
    rows = VT_TILES * tk
    return pl.pallas_call(
        _proj_vt_kernel,
        grid=(b, s // rows),
        in_specs=[pl.BlockSpec((None, rows, d), lambda bi, t: (bi, t, 0)),
                  pl.BlockSpec((n, d), lambda bi, t: (0, 0))],
        out_specs=pl.BlockSpec((None, VT_TILES, n, tk), lambda bi, t: (bi, t, 0, 0)),
        out_shape=jax.ShapeDtypeStruct((b, s // tk, n, tk), BF16),
        compiler_params=_cparams(2),
        name="proj_vt",
    )(h.reshape(b, s, d), wt)


SCORE_TK = 512
COARSE_BITS = 16
BF16_STEP = 1 << 16
FINE_UNCHECKED = 10
N_CHAINS = 4
M_INIT = -1e30


def _key16_to_f32(u):
    key = u ^ 0x8000
    bits = jnp.where((key & 0x8000) != 0, key ^ 0x7FFF, key)
    return pltpu.bitcast(jnp.left_shift(bits, 16), F32)


def _f32_to_key32(x):
    bits = pltpu.bitcast(x, jnp.int32)
    return jnp.where(bits < 0, bits ^ 0x7FFFFFFF, bits)


def _key32_to_f32(key):
    return pltpu.bitcast(jnp.where(key < 0, key ^ 0x7FFFFFFF, key), F32)


def _dsa_kernel(q_ref, iq_ref, k_ref, ikw_ref, vt_ref, ga_ref, o_ref,
                sc_ref, sc16_ref, bias_ref, acc_ref, p_ref, *, topk):
    i = pl.program_id(1)
    tk = ATT_TK
    nk = ((i + 1) * Q_BLOCK + tk - 1) // tk
    q0 = pl.multiple_of(i * Q_BLOCK, Q_BLOCK)
    qidx = q0 + lax.broadcasted_iota(jnp.int32, (1, Q_BLOCK), 1)

    def tile_start(t):
        return pl.multiple_of(t * tk, tk)

    def key_index(k0):
        return k0 + lax.broadcasted_iota(jnp.int32, (tk, 1), 0)

    iq = iq_ref[...]
    n_pair = IDX_HEADS // 2
    lhs = jnp.concatenate(
        [iq[:, p * LANES:(p + 1) * LANES] for p in range(n_pair)], axis=0)
    w = ikw_ref[pl.ds(q0, Q_BLOCK), 2 * LANES:3 * LANES].astype(F32)
    wt = w.T * ((IDX_HEADS ** -0.5) * (IDX_DIM ** -0.5))
    wrow = [wt[h:h + 1, :] for h in range(IDX_HEADS)]

    def score_tile(t, carry):
        k0 = pl.multiple_of(t * SCORE_TK, SCORE_TK)
        acc = jnp.zeros((SCORE_TK, Q_BLOCK), F32)
        for e in range(2):
            ik_e = ikw_ref[pl.ds(k0, SCORE_TK), e * LANES:(e + 1) * LANES]
            d = lax.dot_general(ik_e, lhs, NT_DIMS,
                                preferred_element_type=F32)
            for p in range(n_pair):
                dp = d[:, p * Q_BLOCK:(p + 1) * Q_BLOCK]
                acc = acc + wrow[2 * p + e] * jnp.maximum(dp, 0.0)
        kidx = k0 + lax.broadcasted_iota(jnp.int32, (SCORE_TK, 1), 0)
        sc = jnp.where(kidx <= qidx, acc, NEG_INF)
        sc_ref[pl.ds(k0, SCORE_TK), :] = sc
        sc16_ref[pl.ds(k0, SCORE_TK), :] = sc.astype(BF16)
        return carry

    lax.fori_loop(0, ((i + 1) * Q_BLOCK + SCORE_TK - 1) // SCORE_TK, score_tile, 0)

    one16 = jnp.ones((), BF16)
    zero16 = jnp.zeros((), BF16)
    n_packed = tk // PACKED_SUBLANES

    def coarse(it, t_u):
        cand_u = t_u | jnp.left_shift(jnp.int32(1), COARSE_BITS - 1 - it)
        cand = _key16_to_f32(cand_u).astype(BF16)

        def count_tile(t, cnts):
            x = sc16_ref[pl.ds(tile_start(t), tk), :]
            ind = jnp.where(x >= cand, one16, zero16)
            cnts = list(cnts)
            for j in range(n_packed):
                cnts[j % N_CHAINS] = cnts[j % N_CHAINS] + ind[
                    j * PACKED_SUBLANES:(j + 1) * PACKED_SUBLANES, :]
            return tuple(cnts)

        cnts = lax.fori_loop(
            0, nk, count_tile,
            (jnp.zeros((PACKED_SUBLANES, Q_BLOCK), BF16),) * N_CHAINS)
        total = jnp.sum(sum(c.astype(F32) for c in cnts), axis=0, keepdims=True)
        return jnp.where(total >= topk, cand_u, t_u)

    t_u = lax.fori_loop(0, COARSE_BITS, coarse,
                        jnp.zeros((1, Q_BLOCK), jnp.int32))

    def count_ge(cand):
        def count_tile(t, cnts):
            s = sc_ref[pl.ds(tile_start(t), tk), :]
            ind = jnp.where(s >= cand, 1.0, 0.0)
            cnts = list(cnts)
            for j in range(tk // SUBLANES):
                cnts[j % N_CHAINS] = cnts[j % N_CHAINS] + ind[
                    j * SUBLANES:(j + 1) * SUBLANES, :]
            return tuple(cnts)

        cnts = lax.fori_loop(
            0, nk, count_tile,
            (jnp.zeros((SUBLANES, Q_BLOCK), F32),) * N_CHAINS)
        return jnp.sum(sum(cnts), axis=0, keepdims=True)

    base = _f32_to_key32(_key16_to_f32(t_u))
    lo0 = base - BF16_STEP // 2
    hi0 = base + BF16_STEP + 1

    def fine_cond(st):
        lo, hi, cnt_lo = st
        open_ = jnp.logical_and(cnt_lo != topk, hi - lo > 1)
        return jnp.max(jnp.where(open_, 1.0, 0.0)) > 0.0

    def fine_step(st):
        lo, hi, cnt_lo = st
        mid = lo + jnp.right_shift(hi - lo, 1)
        cnt = count_ge(_key32_to_f32(mid))
        ge = cnt >= topk
        return (jnp.where(ge, mid, lo), jnp.where(ge, hi, mid),
                jnp.where(ge, cnt, cnt_lo))

    st = (lo0, hi0, jnp.full((1, Q_BLOCK), -1.0, F32))
    st = lax.fori_loop(0, FINE_UNCHECKED, lambda _, s_: fine_step(s_), st)
    lo, _, cnt_lo = lax.while_loop(
        fine_cond, lambda s_: fine_step(fine_step(s_)), st)
    full_row = qidx < topk
    thr = jnp.where(full_row, NEG_INF, _key32_to_f32(lo))

    def causal_bias(k0):
        return jnp.where(key_index(k0) <= qidx, 0.0, NEG_INF)

    def write_bias():
        def bias_tile(t, carry):
            k0 = tile_start(t)
            s = sc_ref[pl.ds(k0, tk), :]
            bias_ref[pl.ds(k0, tk), :] = jnp.where(s >= thr, causal_bias(k0), NEG_INF)
            return carry

        lax.fori_loop(0, nk, bias_tile, 0)

    def write_bias_with_ties():
        def count_where(pred):
            def count_tile(t, cnt):
                k0 = tile_start(t)
                ind = jnp.where(pred(sc_ref[pl.ds(k0, tk), :], key_index(k0)), 1.0, 0.0)
                return cnt + jnp.sum(ind, axis=0, keepdims=True)

            return lax.fori_loop(0, nk, count_tile, jnp.zeros((1, Q_BLOCK), F32))

        n_tied = topk - count_where(lambda s, kidx: s > thr)

        def index_step(_, st_):
            lo_i, hi_i = st_
            mid = lo_i + jnp.right_shift(hi_i - lo_i, 1)
            cnt = count_where(
                lambda s, kidx: jnp.logical_and(s == thr, kidx <= mid))
            ge = cnt >= n_tied
            return jnp.where(ge, lo_i, mid), jnp.where(ge, mid, hi_i)

        n_keys = nk * tk
        steps = int(np.ceil(np.log2(sc_ref.shape[0]))) + 1
        _, last = lax.fori_loop(
            0, steps, index_step,
            (jnp.full((1, Q_BLOCK), -1, jnp.int32),
             jnp.full((1, Q_BLOCK), 1, jnp.int32) * (n_keys - 1)))

        def bias_tile(t, carry):
            k0 = tile_start(t)
            s = sc_ref[pl.ds(k0, tk), :]
            keep = jnp.logical_or(
                s > thr, jnp.logical_and(s == thr, key_index(k0) <= last))
            bias_ref[pl.ds(k0, tk), :] = jnp.where(keep, causal_bias(k0), NEG_INF)
            return carry

        lax.fori_loop(0, nk, bias_tile, 0)

    tied = jnp.logical_and(cnt_lo != topk, jnp.logical_not(full_row))
    lax.cond(jnp.max(jnp.where(tied, 1.0, 0.0)) > 0.0,
             write_bias_with_ties, write_bias)

    acc_ref[...] = jnp.zeros(acc_ref.shape, F32)

    def qk_head(t, h, m_old):
        k0 = tile_start(t)
        hs = slice(h * HEAD_DIM, (h + 1) * HEAD_DIM)
        s = lax.dot_general(k_ref[pl.ds(k0, tk), hs], q_ref[:, hs], NT_DIMS,
                            preferred_element_type=F32)
        s = s + bias_ref[pl.ds(k0, tk), :]
        m_new = jnp.maximum(m_old, jnp.max(s, axis=0, keepdims=True))
        p_ref[t % 2, h] = jnp.exp2(s - m_new).astype(BF16)
        return m_new, jnp.exp2(m_old - m_new)

    def pv_head(t, h, alpha):
        pv = jnp.dot(vt_ref[t, h * V_ROWS:(h + 1) * V_ROWS, :], p_ref[t % 2, h],
                     preferred_element_type=F32)
        acc_ref[h] = acc_ref[h] * alpha + pv

    def qk_stage(t, ms):
        pairs = [qk_head(t, h, ms[h]) for h in range(N_HEADS)]
        return tuple(p[0] for p in pairs), tuple(p[1] for p in pairs)

    def pv_stage(t, alphas):
        for h in range(N_HEADS):
            pv_head(t, h, alphas[h])

    def att_tile(t, carry):
        ms, alphas = carry
        pairs = []
        for h in range(N_HEADS):
            pv_head(t - 1, h, alphas[h])
            pairs.append(qk_head(t, h, ms[h]))
        return tuple(p[0] for p in pairs), tuple(p[1] for p in pairs)

    m0 = tuple(jnp.full((1, Q_BLOCK), M_INIT, F32) for _ in range(N_HEADS))
    carry = lax.fori_loop(
        0, (nk - 1) // 2,
        lambda j, c: att_tile(2 * j + 2, att_tile(2 * j + 1, c)),
        qk_stage(0, m0))
    carry = lax.cond((nk - 1) % 2 == 1,
                     lambda c: att_tile(nk - 1, c), lambda c: c, carry)
    pv_stage(nk - 1, carry[1])

    for h in range(N_HEADS):
        hs = slice(h * HEAD_DIM, (h + 1) * HEAD_DIM)
        out_t = acc_ref[h, :HEAD_DIM, :] * (1.0 / acc_ref[h, HEAD_DIM:HEAD_DIM + 1, :])
        gate = ga_ref[:, hs].astype(F32)
        o_ref[:, hs] = (out_t.T * (gate * _sigmoid(gate))).astype(o_ref.dtype)


def _dsa_attention(pr, pv, vt, b, s):
    pr3 = pr.reshape(b, s, pr.shape[1])
    pv3 = pv.reshape(b, s, pv.shape[1])
    topk = min(TOPK_MAX, s // 4)
    w_att = ATTN_WIDTH
    single = pl.Buffered(1)
    return pl.pallas_call(
        functools.partial(_dsa_kernel, topk=topk),
        grid=(b, s // Q_BLOCK),
        in_specs=[
            pl.BlockSpec((None, Q_BLOCK, w_att), lambda bi, i: (bi, i, 0)),
            pl.BlockSpec((None, Q_BLOCK, w_att), lambda bi, i: (bi, i, 2)),
            pl.BlockSpec((None, s, w_att), lambda bi, i: (bi, 0, 1)),
            pl.BlockSpec((None, s, 3 * LANES), lambda bi, i: (bi, 0, 8)),
            pl.BlockSpec((None, s // ATT_TK, N_HEADS * V_ROWS, ATT_TK),
                         lambda bi, i: (bi, 0, 0, 0), pipeline_mode=single),
            pl.BlockSpec((None, Q_BLOCK, w_att), lambda bi, i: (bi, i, 0)),
        ],
        out_specs=pl.BlockSpec((None, Q_BLOCK, w_att), lambda bi, i: (bi, i, 0)),
        out_shape=jax.ShapeDtypeStruct((b, s, w_att), BF16),
        scratch_shapes=[pltpu.VMEM((s, Q_BLOCK), F32),
                        pltpu.VMEM((s, Q_BLOCK), BF16),
                        pltpu.VMEM((s, Q_BLOCK), F32),
                        pltpu.VMEM((N_HEADS, V_ROWS, Q_BLOCK), F32),
                        pltpu.VMEM((2, N_HEADS, ATT_TK, Q_BLOCK), BF16)],
        compiler_params=_cparams(2),
        name="dsa_attention",
    )(pr3, pr3, pr3, pr3, vt, pv3)


RNN_T = 512
N_RNN_TILES = D_RNN // LANES
GATE_WIN = 3


def _gate_window_start(j):
    assert RNN_BLOCK_DIM <= LANES
    return min(max(j - 1, 0), N_RNN_TILES - GATE_WIN)


def _rglru_kernel(xr_ref, gr_ref, cw_ref, cb_ref, wg_ref, bg_ref, lam_ref,
                  o_ref, xp_ref, a_ref, u_ref, h_ref, hc_ref):
    c = pl.program_id(1)
    t_len = RNN_T
    pad = SUBLANES

    @pl.when(c == 0)
    def _():
        xp_ref[0:pad, :] = jnp.zeros((pad, D_RNN), F32)
        hc_ref[...] = jnp.zeros((1, D_RNN), F32)

    x = xr_ref[...].astype(F32)
    xp_ref[pad:pad + t_len, :] = x
    y = cb_ref[...] + cw_ref[CONV_WIDTH - 1:CONV_WIDTH, :] * x
    for j in range(CONV_WIDTH - 1):
        back = CONV_WIDTH - 1 - j
        y = y + cw_ref[j:j + 1, :] * xp_ref[pad - back:pad - back + t_len, :]
    xp_ref[0:pad, :] = x[t_len - pad:, :]

    y16 = y.astype(BF16)
    yh = 0.5 * y
    z = -lam_ref[...]
    softplus = jnp.maximum(z, 0.0) + jnp.log1p(jnp.exp(-jnp.abs(z)))
    k = (-0.5 * LRU_C * LOG2E) * softplus
    for j in range(N_RNN_TILES):
        ls = slice(j * LANES, (j + 1) * LANES)
        w0 = _gate_window_start(j) * LANES
        g = jnp.dot(y16[:, w0:w0 + GATE_WIN * LANES], wg_ref[j],
                    preferred_element_type=F32) + bg_ref[j]
        t_r = jnp.tanh(g[:, :LANES])
        t_i = jnp.tanh(g[:, LANES:])
        a = jnp.exp2(k[:, ls] * t_r + k[:, ls])
        a_ref[:, ls] = a
        om = 1.0 - a * a
        root = jnp.where(om > 0.0, om * lax.rsqrt(om), 0.0)
        u_ref[:, ls] = root * (t_i * yh[:, ls] + yh[:, ls])

    def step(t, h):
        h = a_ref[pl.ds(t, 1), :] * h + u_ref[pl.ds(t, 1), :]
        h_ref[pl.ds(t, 1), :] = h
        return h

    hc_ref[...] = lax.fori_loop(0, t_len, step, hc_ref[...], unroll=8)
    gh = 0.5 * gr_ref[...].astype(F32)
    o_ref[...] = (h_ref[...] * (gh * jnp.tanh(gh) + gh)).astype(o_ref.dtype)


def _rglru(pc, b, s, conv_w, conv_b, w_gates, b_gates, lam):
    pc3 = pc.reshape(b, s, pc.shape[1])
    t = RNN_T
    const = lambda bi, c: (0, 0)
    const3 = lambda bi, c: (0, 0, 0)
    return pl.pallas_call(
        _rglru_kernel,
        grid=(b, s // t),
        in_specs=[
            pl.BlockSpec((None, t, D_RNN), lambda bi, c: (bi, c, 0)),
            pl.BlockSpec((None, t, D_RNN), lambda bi, c: (bi, c, 1)),
            pl.BlockSpec((CONV_WIDTH, D_RNN), const),
            pl.BlockSpec((1, D_RNN), const),
            pl.BlockSpec((N_RNN_TILES, GATE_WIN * LANES, 2 * LANES), const3),
            pl.BlockSpec((N_RNN_TILES, 1, 2 * LANES), const3),
            pl.BlockSpec((1, D_RNN), const),
        ],
        out_specs=pl.BlockSpec((None, t, D_RNN), lambda bi, c: (bi, c, 0)),
        out_shape=jax.ShapeDtypeStruct((b, s, D_RNN), BF16),
        scratch_shapes=[pltpu.VMEM((t + SUBLANES, D_RNN), F32),
                        pltpu.VMEM((t, D_RNN), F32),
                        pltpu.VMEM((t, D_RNN), F32),
                        pltpu.VMEM((t, D_RNN), F32),
                        pltpu.VMEM((1, D_RNN), F32)],
        compiler_params=_cparams(2),
        name="rglru",
    )(pc3, pc3, conv_w, conv_b.reshape(1, D_RNN), w_gates, b_gates,
      lam.reshape(1, D_RNN))


def _pack_gates(w_rg_l, b_rg_l, w_ig_l, b_ig_l):
    dense = [jax.scipy.linalg.block_diag(*w) for w in (w_rg_l, w_ig_l)]
    bands, biases = [], []
    for j in range(N_RNN_TILES):
        w0 = _gate_window_start(j) * LANES
        cols = slice(j * LANES, (j + 1) * LANES)
        bands.append(jnp.concatenate(
            [dm[w0:w0 + GATE_WIN * LANES, cols] for dm in dense], axis=1))
        biases.append(jnp.concatenate([b_rg_l[cols], b_ig_l[cols]])[None, :])
    return (0.5 * jnp.stack(bands)).astype(BF16), 0.5 * jnp.stack(biases)


def _merge_kernel(at_ref, hr_ref, ma_ref, mb_ref, x_ref, woa_ref, wor_ref,
                  wo_ref, g_ref, *out_refs):
    y_a = jnp.dot(at_ref[...], woa_ref[...], preferred_element_type=F32)
    y_b = jnp.dot(hr_ref[...], wor_ref[...], preferred_element_type=F32)
    merged = (_sigmoid(ma_ref[...].astype(F32)) * y_a
              + _sigmoid(mb_ref[...].astype(F32)) * y_b)
    x = x_ref[...] + jnp.dot(merged.astype(BF16), wo_ref[...],
                             preferred_element_type=F32)
    no_ref = out_refs[-1]
    if len(out_refs) == 2:
        out_refs[0][...] = x
    y = x * lax.rsqrt(jnp.mean(x * x, axis=-1, keepdims=True) + NORM_EPS)
    no_ref[...] = (y * g_ref[...]).astype(no_ref.dtype)


def _merge(attn_g, hr_g, pv, x2d, w_oa, w_or, w_o, g, last):
    m, d = x2d.shape
    tm = 1024
    const = lambda i: (0, 0)
    row = pl.BlockSpec((tm, d), lambda i: (i, 0))
    normed = jax.ShapeDtypeStruct((m, d), F32 if last else BF16)
    outs = pl.pallas_call(
        _merge_kernel,
        grid=(m // tm,),
        in_specs=[
            pl.BlockSpec((tm, ATTN_WIDTH), lambda i: (i, 0)),
            pl.BlockSpec((tm, D_RNN), lambda i: (i, 0)),
            pl.BlockSpec((tm, d), lambda i: (i, 1)),
            pl.BlockSpec((tm, d), lambda i: (i, 2)),
            row,
            pl.BlockSpec((ATTN_WIDTH, d), const),
            pl.BlockSpec((D_RNN, d), const),
            pl.BlockSpec((d, d), const),
            pl.BlockSpec((1, d), const),
        ],
        out_specs=[row] if last else [row, row],
        out_shape=[normed] if last else [jax.ShapeDtypeStruct((m, d), F32), normed],
        compiler_params=_cparams(1),
        name="merge",
    )(attn_g, hr_g, pv, pv, x2d, w_oa, w_or, w_o, g.reshape(1, d))
    return (None, outs[0]) if last else tuple(outs)


def _pack_weights(w_in):
    sizes = (ATTN_WIDTH,) * 4 + (IDX_HEADS * IDX_DIM, IDX_DIM, IDX_HEADS,
                                 D_RNN, D_RNN, D_MODEL, D_MODEL)
    (o_q, o_k, o_v, o_ga, o_iq, o_ik, o_iw, o_xr, o_gr, o_ma,
     o_mb, o_end) = np.concatenate([[0], np.cumsum(sizes)]).tolist()
    w16 = w_in.astype(BF16)
    n_l, d = w_in.shape[:2]
    half = IDX_DIM // 2
    z = lambda n: jnp.zeros((n_l, d, n), BF16)
    iq = w16[:, :, o_iq:o_ik].reshape(n_l, d, IDX_HEADS // 2, 2, 2, half).transpose(
        0, 1, 2, 4, 3, 5).reshape(n_l, d, IDX_HEADS * IDX_DIM)
    ik1, ik2 = w16[:, :, o_ik:o_ik + half], w16[:, :, o_ik + half:o_iw]
    w_rope = jnp.concatenate(
        [w16[:, :, o_q:o_v], iq,
         ik1, z(half), ik2, z(half), z(half), ik1, z(half), ik2,
         w16[:, :, o_iw:o_xr], z(ROPE_TN - 2 * LANES - IDX_HEADS)], axis=2)
    w_gate = jnp.concatenate([w16[:, :, o_ga:o_iq], w16[:, :, o_ma:o_end]], axis=2)
    vt = jnp.swapaxes(w16[:, :, o_v:o_ga], 1, 2).reshape(n_l, N_HEADS, HEAD_DIM, d)
    w_vt = jnp.pad(vt, ((0, 0), (0, 0), (0, V_ROWS - HEAD_DIM), (0, 0))).reshape(
        n_l, N_HEADS * V_ROWS, d)
    w_rnn = w16[:, :, o_xr:o_ma]
    return w_rope, w_gate, w_vt, w_rnn


def kernel(x, positions, norm_g, w_in, conv_w, conv_b, w_rg, b_rg, w_ig, b_ig,
           lru_lambda, w_out_attn, w_out_rnn, w_o, final_g):
    b, s, d = x.shape
    depth = w_in.shape[0]
    m = b * s
    cos, sin = _rope_tables(positions)
    x2d = x.reshape(m, d)
    h = _rmsnorm(x2d, norm_g[0], BF16)
    out = None
    w_rope, w_gate, w_vt, w_rnn = _pack_weights(w_in)
    w_oa, w_or, w_oo = (w.astype(BF16) for w in (w_out_attn, w_out_rnn, w_o))
    for l in range(depth):
        w_gates, b_gates = _pack_gates(w_rg[l], b_rg[l], w_ig[l], b_ig[l])
        pr = _proj_rope(h, w_rope[l], cos, sin)
        pv = _proj_plain(h, w_gate[l], 4096, 1024, "proj_gate")
        vt = _proj_vt(h, w_vt[l], b, s)
        pc = _proj_plain(h, w_rnn[l], 2048, D_RNN, "proj_rnn")
        attn_g = _dsa_attention(pr, pv, vt, b, s).reshape(m, ATTN_WIDTH)
        hr_g = _rglru(pc, b, s, conv_w[l], conv_b[l], w_gates, b_gates,
                      lru_lambda[l]).reshape(m, D_RNN)
        last = l == depth - 1
        g_next = final_g if last else norm_g[l + 1]
        x2d, normed = _merge(attn_g, hr_g, pv, x2d, w_oa[l], w_or[l], w_oo[l],
                             g_next, last)
        if last:
            out = normed
        else:
            h = normed
    return out.reshape(b, s, d)
```
